```python
import jax, jax.numpy as jnp
from jax import lax
import numpy as np

D_MODEL = 1024
BATCH = 4
SEQ = 8192
DEPTH = 4

N_A_LAYERS = DEPTH // 2
N_B_LAYERS = DEPTH - N_A_LAYERS

LRU_WIDTH = D_MODEL * 3 // 2
LRU_BLOCKS = 16
LRU_BLOCK_W = LRU_WIDTH // LRU_BLOCKS
CONV_WIDTH = 4
LRU_C = 8.0

N_Q_HEADS = 16
N_KV_HEADS = 2
GROUP = N_Q_HEADS // N_KV_HEADS
HEAD_DIM = 64
ATT_WIDTH = N_Q_HEADS * HEAD_DIM
WINDOW = 128
BLOCK = 128

EPS = 1e-6

kernel_name = "yoco_rglru_swa_sink_alibi_trunk"


def rmsnorm(x, g):
    x32 = x.astype(jnp.float32)
    y = x32 * lax.rsqrt(jnp.mean(x32 * x32, axis=-1, keepdims=True) + EPS)
    return (y * g.astype(jnp.float32)).astype(x.dtype)


def causal_depthwise_conv(u, w, b):
    r = u.shape[-1]
    out = lax.conv_general_dilated(
        u, w.astype(u.dtype)[:, None, :], window_strides=(1,),
        padding=[(CONV_WIDTH - 1, 0)],
        dimension_numbers=("NWC", "WIO", "NWC"),
        feature_group_count=r)
    return out + b.astype(u.dtype)


def _linear_combine(c1, c2):
    a1, b1 = c1
    a2, b2 = c2
    return a1 * a2, a2 * b1 + b2


def rglru_layer(x, norm_g, w_in, conv_w, conv_b, wr, br, wi, bi, lam, w_out):
    bsz, seq, _ = x.shape
    h = rmsnorm(x, norm_g)
    u = h @ w_in
    xb, gate = u[..., :LRU_WIDTH], u[..., LRU_WIDTH:]
    xb = causal_depthwise_conv(xb, conv_w, conv_b)
    xs = xb.reshape(bsz, seq, LRU_BLOCKS, LRU_BLOCK_W)
    r = jax.nn.sigmoid((jnp.einsum('bshi,hij->bshj', xs, wr).reshape(bsz, seq, LRU_WIDTH) + br).astype(jnp.float32))
    i = jax.nn.sigmoid((jnp.einsum('bshi,hij->bshj', xs, wi).reshape(bsz, seq, LRU_WIDTH) + bi).astype(jnp.float32))
    log_a = -LRU_C * r * jax.nn.softplus(-lam.astype(jnp.float32))
    a = jnp.exp(log_a)
    b = jnp.sqrt(-jnp.expm1(2.0 * log_a)) * (i * xb.astype(jnp.float32))
    _, hs = lax.associative_scan(_linear_combine, (a, b), axis=1)
    y = hs.astype(x.dtype) * jax.nn.silu(gate)
    return x + y @ w_out


def sliding_window_sink_attention(q, k, v, sinks):
    bsz, seq = q.shape[0], q.shape[1]
    nb = seq // BLOCK
    qb = q.reshape(bsz, nb, BLOCK, N_KV_HEADS, GROUP, HEAD_DIM)

    def band(t):
        cur = t.reshape(bsz, nb, BLOCK, N_KV_HEADS, HEAD_DIM)
        prev = jnp.concatenate([jnp.zeros_like(cur[:, :1]), cur[:, :-1]], axis=1)
        return jnp.concatenate([prev, cur], axis=2)

    kb, vb = band(k), band(v)
    scale = HEAD_DIM ** -0.5
    s = jnp.einsum('bnqkgd,bnskd->bnkgqs', qb, kb).astype(jnp.float32) * scale

    qi = jnp.arange(BLOCK)[:, None] + BLOCK
    kj = jnp.arange(2 * BLOCK)[None, :]
    dist = qi - kj
    abs_k = jnp.arange(nb)[:, None, None] * BLOCK - BLOCK + kj[None]
    valid = (dist >= 0)[None] & (dist < WINDOW)[None] & (abs_k >= 0)

    slopes = jnp.exp2(-8.0 * jnp.arange(1, N_Q_HEADS + 1, dtype=jnp.float32) / N_Q_HEADS)
    slopes = slopes.reshape(N_KV_HEADS, GROUP)
    alibi = -slopes[:, :, None, None] * dist.astype(jnp.float32)[None, None]

    s = jnp.where(valid[None, :, None, None], s + alibi[None, None], -jnp.inf)
    sink = sinks.astype(jnp.float32).reshape(N_KV_HEADS, GROUP)[None, None, :, :, None, None]
    m = jnp.maximum(jnp.max(s, axis=-1, keepdims=True), sink)
    p = jnp.exp(s - m)
    denom = jnp.sum(p, axis=-1, keepdims=True) + jnp.exp(sink - m)
    p = (p / denom).astype(v.dtype)
    o = jnp.einsum('bnkgqs,bnskd->bnqkgd', p, vb)
    return o.reshape(bsz, seq, ATT_WIDTH)


def swa_layer(x, k, v, norm_g, w_in, q_norm_g, sinks, w_out):
    bsz, seq, _ = x.shape
    h = rmsnorm(x, norm_g)
    u = h @ w_in
    q, gate = u[..., :ATT_WIDTH], u[..., ATT_WIDTH:]
    q = rmsnorm(q.reshape(bsz, seq, N_Q_HEADS, HEAD_DIM), q_norm_g)
    o = sliding_window_sink_attention(q, k, v, sinks)
    y = o * jax.nn.silu(gate)
    return x + y @ w_out


def setup_inputs(seed: int = 0) -> dict:
    key = jax.random.key(seed)
    ks = jax.random.split(key, 20)
    nA, nB, R, D = N_A_LAYERS, N_B_LAYERS, LRU_WIDTH, D_MODEL
    f32 = jnp.float32
    nrm = lambda k, shape, s: jax.random.normal(k, shape, f32) * s
    x = jax.random.normal(ks[0], (BATCH, SEQ, D), f32)
    a_norm_g = 1.0 + nrm(ks[1], (nA, D), 0.02)
    a_w_in = nrm(ks[2], (nA, D, 2 * R), D ** -0.5)
    a_conv_w = nrm(ks[3], (nA, CONV_WIDTH, R), CONV_WIDTH ** -0.5)
    a_conv_b = nrm(ks[4], (nA, R), 0.01)
    a_gate_r_w = nrm(ks[5], (nA, LRU_BLOCKS, LRU_BLOCK_W, LRU_BLOCK_W), LRU_BLOCK_W ** -0.5)
    a_gate_r_b = nrm(ks[6], (nA, R), 0.01)
    a_gate_i_w = nrm(ks[7], (nA, LRU_BLOCKS, LRU_BLOCK_W, LRU_BLOCK_W), LRU_BLOCK_W ** -0.5)
    a_gate_i_b = nrm(ks[8], (nA, R), 0.01)
    a8 = jax.random.uniform(ks[9], (nA, R), f32, 0.9, 0.999)
    a0 = a8 ** (1.0 / LRU_C)
    a_lambda = jnp.log(a0) - jnp.log1p(-a0)
    a_w_out = nrm(ks[10], (nA, R, D), R ** -0.5)
    kv_norm_g = 1.0 + nrm(ks[11], (D,), 0.02)
    w_kv = nrm(ks[12], (D, 2 * N_KV_HEADS * HEAD_DIM), D ** -0.5)
    k_norm_g = 1.0 + nrm(ks[13], (HEAD_DIM,), 0.02)
    b_norm_g = 1.0 + nrm(ks[14], (nB, D), 0.02)
    b_w_in = nrm(ks[15], (nB, D, 2 * ATT_WIDTH), D ** -0.5)
    q_norm_g = 1.0 + nrm(ks[16], (nB, HEAD_DIM), 0.02)
    sinks = nrm(ks[17], (nB, N_Q_HEADS), 0.5)
    b_w_out = nrm(ks[18], (nB, ATT_WIDTH, D), ATT_WIDTH ** -0.5)
    return {"x": x, "a_norm_g": a_norm_g, "a_w_in": a_w_in, "a_conv_w": a_conv_w,
            "a_conv_b": a_conv_b, "a_gate_r_w": a_gate_r_w, "a_gate_r_b": a_gate_r_b,
            "a_gate_i_w": a_gate_i_w, "a_gate_i_b": a_gate_i_b, "a_lambda": a_lambda,
            "a_w_out": a_w_out, "kv_norm_g": kv_norm_g, "w_kv": w_kv, "k_norm_g": k_norm_g,
            "b_norm_g": b_norm_g, "b_w_in": b_w_in, "q_norm_g": q_norm_g, "sinks": sinks,
            "b_w_out": b_w_out}


def reference(x, a_norm_g, a_w_in, a_conv_w, a_conv_b, a_gate_r_w, a_gate_r_b,
              a_gate_i_w, a_gate_i_b, a_lambda, a_w_out, kv_norm_g, w_kv, k_norm_g,
              b_norm_g, b_w_in, q_norm_g, sinks, b_w_out):
    bsz, seq, _ = x.shape
    k = v = None
    for layer in range(DEPTH):
        if layer < N_A_LAYERS:
            l = layer
            x = rglru_layer(x, a_norm_g[l], a_w_in[l], a_conv_w[l], a_conv_b[l],
                            a_gate_r_w[l], a_gate_r_b[l], a_gate_i_w[l], a_gate_i_b[l],
                            a_lambda[l], a_w_out[l])
            if layer == N_A_LAYERS - 1:
                kv = rmsnorm(x, kv_norm_g) @ w_kv
                kv = kv.reshape(bsz, seq, 2, N_KV_HEADS, HEAD_DIM)
                k = rmsnorm(kv[:, :, 0], k_norm_g)
                v = kv[:, :, 1]
        else:
            l = layer - N_A_LAYERS
            x = swa_layer(x, k, v, b_norm_g[l], b_w_in[l], q_norm_g[l], sinks[l], b_w_out[l])
    return x
```

```python
import functools

import jax
import jax.numpy as jnp
from jax import lax
from jax.experimental import pallas as pl
from jax.experimental.pallas import tpu as pltpu

D_MODEL = 1024
LRU_WIDTH = 1536
LRU_BLOCKS = 16
LRU_BLOCK_W = LRU_WIDTH // LRU_BLOCKS
CONV_WIDTH = 4
LRU_C = 8.0
N_Q_HEADS = 16
N_KV_HEADS = 2
GROUP = N_Q_HEADS // N_KV_HEADS
HEAD_DIM = 64
ATT_WIDTH = N_Q_HEADS * HEAD_DIM
WINDOW = 128
BLOCK = 128
EPS = 1e-6

SUBLANES = 8
LANES = 128
GATE_GROUP_BLOCKS = 4
GATE_GROUP_W = GATE_GROUP_BLOCKS * LRU_BLOCK_W
N_GATE_GROUPS = LRU_BLOCKS // GATE_GROUP_BLOCKS
PAIRS_PER_KV = GROUP // 2

RGLRU_CHUNK = 256
KV_CHUNK = 512
SWA_CHUNK = 256
VMEM_LIMIT_BYTES = 56 * 1024 * 1024

F32 = jnp.float32
BF16 = jnp.bfloat16


def _rmsnorm_rows(x, g):
    ms = jnp.mean(x * x, axis=-1, keepdims=True)
    return x * lax.rsqrt(ms + EPS) * g


def _sigmoid(z):
    return 0.5 * (1.0 + jnp.tanh(0.5 * z))


def _softplus(z):
    return jnp.maximum(z, 0.0) + jnp.log1p(jnp.exp(-jnp.abs(z)))


def _scan_rows(a, b, h0):
    t, c = a.shape
    groups = t // SUBLANES
    a3 = a.reshape(groups, SUBLANES, c)
    b3 = b.reshape(groups, SUBLANES, c)
    row = lax.broadcasted_iota(jnp.int32, (groups, SUBLANES, c), 1)
    shift = 1
    while shift < SUBLANES:
        keep = row >= shift
        a_prev = jnp.where(keep, pltpu.roll(a3, shift, axis=1), 1.0)
        b_prev = jnp.where(keep, pltpu.roll(b3, shift, axis=1), 0.0)
        b3 = a3 * b_prev + b3
        a3 = a3 * a_prev
        shift *= 2
    outs = []
    h = h0
    for g in range(groups):
        o = a3[g] * h + b3[g]
        outs.append(o)
        h = o[SUBLANES - 1:SUBLANES, :]
    return jnp.concatenate(outs, axis=0), h


def _rglru_kernel(x_ref, g_ref, win_ref, convw_ref, convb_ref, wg_ref, br_ref, bi_ref,
                  lam_ref, wout_ref, o_ref, ubuf_ref, hcar_ref):
    chunk = x_ref.shape[1]
    r_w = LRU_WIDTH

    @pl.when(pl.program_id(1) == 0)
    def _():
        ubuf_ref[0:SUBLANES, :] = jnp.zeros((SUBLANES, r_w), F32)
        hcar_ref[...] = jnp.zeros_like(hcar_ref)

    x = x_ref[0]
    h = _rmsnorm_rows(x, g_ref[...]).astype(BF16)
    u = jnp.dot(h, win_ref[...], preferred_element_type=F32)
    ubuf_ref[SUBLANES:SUBLANES + chunk, :] = u[:, :r_w]
    gate = u[:, r_w:]

    xb = convb_ref[...] + convw_ref[CONV_WIDTH - 1:CONV_WIDTH, :] * u[:, :r_w]
    for k in range(CONV_WIDTH - 1):
        back = CONV_WIDTH - 1 - k
        xb = xb + convw_ref[k:k + 1, :] * ubuf_ref[SUBLANES - back:SUBLANES - back + chunk, :]
    ubuf_ref[0:SUBLANES, :] = ubuf_ref[chunk:chunk + SUBLANES, :]

    log_a_scale = -LRU_C * _softplus(-lam_ref[...])
    ys = []
    for grp in range(N_GATE_GROUPS):
        lo = grp * GATE_GROUP_W
        hi = lo + GATE_GROUP_W
        xg = xb[:, lo:hi]
        ri = jnp.dot(xg.astype(BF16), wg_ref[grp], preferred_element_type=F32)
        r = _sigmoid(ri[:, :GATE_GROUP_W] + br_ref[:, lo:hi])
        i = _sigmoid(ri[:, GATE_GROUP_W:] + bi_ref[:, lo:hi])
        log_a = r * log_a_scale[:, lo:hi]
        a = jnp.exp(log_a)
        b = jnp.sqrt(-jnp.tanh(log_a) * (a * a + 1.0)) * (i * xg)
        hs, h_last = _scan_rows(a, b, hcar_ref[:, lo:hi])
        hcar_ref[:, lo:hi] = h_last
        gt = gate[:, lo:hi]
        ys.append((hs * (gt * _sigmoid(gt))).astype(BF16))
    y = jnp.concatenate(ys, axis=1)
    o_ref[0] = x + jnp.dot(y, wout_ref[...], preferred_element_type=F32)


def _rglru_layer(x, norm_g, w_in, conv_w, conv_b, w_gates, br, bi, lam, w_out):
    bsz, seq, d = x.shape
    chunk = RGLRU_CHUNK
    const = lambda *shape: pl.BlockSpec(shape, lambda b, t: (0,) * len(shape))
    return pl.pallas_call(
        _rglru_kernel,
        grid=(bsz, seq // chunk),
        in_specs=[
            pl.BlockSpec((1, chunk, d), lambda b, t: (b, t, 0)),
            const(1, d),
            const(d, 2 * LRU_WIDTH),
            const(CONV_WIDTH, LRU_WIDTH),
            const(1, LRU_WIDTH),
            const(N_GATE_GROUPS, GATE_GROUP_W, 2 * GATE_GROUP_W),
            const(1, LRU_WIDTH),
            const(1, LRU_WIDTH),
            const(1, LRU_WIDTH),
            const(LRU_WIDTH, d),
        ],
        out_specs=pl.BlockSpec((1, chunk, d), lambda b, t: (b, t, 0)),
        out_shape=jax.ShapeDtypeStruct(x.shape, x.dtype),
        scratch_shapes=[
            pltpu.VMEM((chunk + SUBLANES, LRU_WIDTH), F32),
            pltpu.VMEM((1, LRU_WIDTH), F32),
        ],
        compiler_params=pltpu.CompilerParams(
            dimension_semantics=("arbitrary", "arbitrary"),
            vmem_limit_bytes=VMEM_LIMIT_BYTES),
        name="rglru_layer",
    )(x, norm_g, w_in, conv_w, conv_b, w_gates, br, bi, lam, w_out)


def _pair_layout(t):
    lane = lax.broadcasted_iota(jnp.int32, t.shape, 1)
    low = lane < HEAD_DIM
    swapped = pltpu.roll(t, HEAD_DIM, axis=1)
    zero = jnp.zeros_like(t)
    return jnp.concatenate([
        jnp.where(low, t, zero), jnp.where(low, zero, swapped),
        jnp.where(low, swapped, zero), jnp.where(low, zero, t)], axis=1)


def _kv_kernel(x_ref, g_ref, wkv_ref, kg_ref, k_ref, v_ref):
    x = x_ref[0]
    h = _rmsnorm_rows(x, g_ref[...]).astype(BF16)
    kv = jnp.dot(h, wkv_ref[...], preferred_element_type=F32)
    kw = N_KV_HEADS * HEAD_DIM
    k = kv[:, :kw]
    v = kv[:, kw:]
    lane = lax.broadcasted_iota(jnp.int32, k.shape, 1)
    low = lane < HEAD_DIM
    k2 = k * k
    s_low = jnp.sum(jnp.where(low, k2, 0.0), axis=-1, keepdims=True)
    s_all = jnp.sum(k2, axis=-1, keepdims=True)
    ms = jnp.where(low, s_low, s_all - s_low) * (1.0 / HEAD_DIM)
    kn = k * lax.rsqrt(ms + EPS) * kg_ref[...] * (HEAD_DIM ** -0.5)
    k_ref[0] = _pair_layout(kn).astype(BF16)
    v_ref[0] = _pair_layout(v).astype(BF16)


def _shared_kv(x, kv_norm_g, w_kv, k_norm_g2):
    bsz, seq, d = x.shape
    chunk = KV_CHUNK
    kw = N_KV_HEADS * HEAD_DIM
    out_w = 2 * N_KV_HEADS * LANES
    const = lambda *shape: pl.BlockSpec(shape, lambda b, t: (0,) * len(shape))
    return pl.pallas_call(
        _kv_kernel,
        grid=(bsz, seq // chunk),
        in_specs=[
            pl.BlockSpec((1, chunk, d), lambda b, t: (b, t, 0)),
            const(1, d),
            const(d, 2 * kw),
            const(1, kw),
        ],
        out_specs=[pl.BlockSpec((1, chunk, out_w), lambda b, t: (b, t, 0))] * 2,
        out_shape=[jax.ShapeDtypeStruct((bsz, seq, out_w), BF16)] * 2,
        compiler_params=pltpu.CompilerParams(
            dimension_semantics=("arbitrary", "arbitrary"),
            vmem_limit_bytes=VMEM_LIMIT_BYTES),
        name="shared_kv",
    )(x, kv_norm_g, w_kv, k_norm_g2)


def _swa_kernel(sinks_ref, x_ref, g_ref, win_ref, qg_ref, kprev_ref, kcur_ref, vprev_ref,
                vcur_ref, wout_ref, o_ref):
    chunk = x_ref.shape[1]
    first_chunk = pl.program_id(1) == 0

    x = x_ref[0]
    h = _rmsnorm_rows(x, g_ref[...]).astype(BF16)
    u = jnp.dot(h, win_ref[...], preferred_element_type=F32)
    gate = u[:, ATT_WIDTH:]

    lane = lax.broadcasted_iota(jnp.int32, (chunk, LANES), 1)
    low = lane < HEAD_DIM
    qn = []
    for p in range(N_Q_HEADS // 2):
        q = u[:, p * LANES:(p + 1) * LANES]
        q2 = q * q
        s_low = jnp.sum(jnp.where(low, q2, 0.0), axis=-1, keepdims=True)
        s_all = jnp.sum(q2, axis=-1, keepdims=True)
        ms = jnp.where(low, s_low, s_all - s_low) * (1.0 / HEAD_DIM)
        qn.append((q * lax.rsqrt(ms + EPS) * qg_ref[...]).astype(BF16))

    qi = lax.broadcasted_iota(jnp.int32, (BLOCK, BLOCK), 0)
    kc = lax.broadcasted_iota(jnp.int32, (BLOCK, BLOCK), 1)
    from_prev = kc > qi
    dist = ((qi - kc) & (BLOCK - 1)).astype(F32)
    low_o = lax.broadcasted_iota(jnp.int32, (BLOCK, LANES), 1) < HEAD_DIM

    o_blocks = []
    for n in range(chunk // BLOCK):
        rows = slice(n * BLOCK, (n + 1) * BLOCK)
        o_pairs = []
        for j in range(N_KV_HEADS):
            q_stack = jnp.concatenate(
                [qn[j * PAIRS_PER_KV + i][rows] for i in range(PAIRS_PER_KV)], axis=0)
            p_stack = []
            inv_den = []
            for e in range(2):
                cols = slice((2 * j + e) * LANES, (2 * j + e + 1) * LANES)
                if n == 0:
                    k_band = jnp.concatenate([kprev_ref[0, :, cols], kcur_ref[0, 0:BLOCK, cols]],
                                             axis=0)
                else:
                    k_band = kcur_ref[0, (n - 1) * BLOCK:(n + 1) * BLOCK, cols]
                s = lax.dot_general(q_stack, k_band, (((1,), (1,)), ((), ())),
                                    preferred_element_type=F32)
                p_e = []
                inv_e = []
                for i in range(PAIRS_PER_KV):
                    head = j * GROUP + 2 * i + e
                    slope = 2.0 ** (-8.0 * (head + 1) / N_Q_HEADS)
                    s_h = s[i * BLOCK:(i + 1) * BLOCK]
                    sc = jnp.where(from_prev, s_h[:, :BLOCK], s_h[:, BLOCK:]) - slope * dist
                    if n == 0:
                        sc = jnp.where(jnp.logical_and(from_prev, first_chunk), -jnp.inf, sc)
                    sink = sinks_ref[head]
                    m = jnp.maximum(jnp.max(sc, axis=-1, keepdims=True), sink)
                    pr = jnp.exp(sc - m)
                    den = jnp.sum(pr, axis=-1, keepdims=True) + jnp.exp(sink - m)
                    inv_e.append(1.0 / den)
                    zero = jnp.zeros_like(pr)
                    p_e.append(jnp.concatenate(
                        [jnp.where(from_prev, pr, zero), jnp.where(from_prev, zero, pr)],
                        axis=1).astype(BF16))
                p_stack.append(jnp.concatenate(p_e, axis=0))
                inv_den.append(inv_e)
            acc = None
            for e in range(2):
                cols = slice((2 * j + e) * LANES, (2 * j + e + 1) * LANES)
                if n == 0:
                    v_band = jnp.concatenate([vprev_ref[0, :, cols], vcur_ref[0, 0:BLOCK, cols]],
                                             axis=0)
                else:
                    v_band = vcur_ref[0, (n - 1) * BLOCK:(n + 1) * BLOCK, cols]
                part = jnp.dot(p_stack[e], v_band, preferred_element_type=F32)
                acc = part if acc is None else acc + part
            for i in range(PAIRS_PER_KV):
                inv = jnp.where(low_o, inv_den[0][i], inv_den[1][i])
                o_pairs.append(acc[i * BLOCK:(i + 1) * BLOCK] * inv)
        o_blocks.append(jnp.concatenate(o_pairs, axis=1))
    o = jnp.concatenate(o_blocks, axis=0)
    y = (o * (gate * _sigmoid(gate))).astype(BF16)
    o_ref[0] = x + jnp.dot(y, wout_ref[...], preferred_element_type=F32)


def _swa_layer(x, k4, v4, norm_g, w_in, q_norm_g2, sinks, w_out):
    bsz, seq, d = x.shape
    chunk = SWA_CHUNK
    blocks_per_chunk = chunk // BLOCK
    kv_w = k4.shape[-1]
    const = lambda *shape: pl.BlockSpec(shape, lambda b, t, s: (0,) * len(shape))
    prev_spec = pl.BlockSpec(
        (1, BLOCK, kv_w), lambda b, t, s: (b, jnp.maximum(t * blocks_per_chunk - 1, 0), 0))
    cur_spec = pl.BlockSpec((1, chunk, kv_w), lambda b, t, s: (b, t, 0))
    grid_spec = pltpu.PrefetchScalarGridSpec(
        num_scalar_prefetch=1,
        grid=(bsz, seq // chunk),
        in_specs=[
            pl.BlockSpec((1, chunk, d), lambda b, t, s: (b, t, 0)),
            const(1, d),
            const(d, 2 * ATT_WIDTH),
            const(1, LANES),
            prev_spec, cur_spec, prev_spec, cur_spec,
            const(ATT_WIDTH, d),
        ],
        out_specs=pl.BlockSpec((1, chunk, d), lambda b, t, s: (b, t, 0)),
    )
    return pl.pallas_call(
        _swa_kernel,
        grid_spec=grid_spec,
        out_shape=jax.ShapeDtypeStruct(x.shape, x.dtype),
        compiler_params=pltpu.CompilerParams(
            dimension_semantics=("arbitrary", "arbitrary"),
            vmem_limit_bytes=VMEM_LIMIT_BYTES),
        name="swa_layer",
    )(sinks, x, norm_g, w_in, q_norm_g2, k4, k4, v4, v4, w_out)


def _gate_group_weights(wr, wi):
    def blockdiag(w):
        w = w.reshape(N_GATE_GROUPS, GATE_GROUP_BLOCKS, LRU_BLOCK_W, LRU_BLOCK_W)
        eye = jnp.eye(GATE_GROUP_BLOCKS, dtype=w.dtype)
        full = w[:, :, :, None, :] * eye[None, :, None, :, None]
        return full.reshape(N_GATE_GROUPS, GATE_GROUP_W, GATE_GROUP_W)
    return jnp.concatenate([blockdiag(wr), blockdiag(wi)], axis=-1).astype(BF16)


def kernel(x, a_norm_g, a_w_in, a_conv_w, a_conv_b, a_gate_r_w, a_gate_r_b, a_gate_i_w,
           a_gate_i_b, a_lambda, a_w_out, kv_norm_g, w_kv, k_norm_g, b_norm_g, b_w_in,
           q_norm_g, sinks, b_w_out):
    n_a = a_w_in.shape[0]
    n_b = b_w_in.shape[0]
    for l in range(n_a):
        x = _rglru_layer(
            x, a_norm_g[l][None], a_w_in[l].astype(BF16), a_conv_w[l], a_conv_b[l][None],
            _gate_group_weights(a_gate_r_w[l], a_gate_i_w[l]), a_gate_r_b[l][None],
            a_gate_i_b[l][None], a_lambda[l][None], a_w_out[l].astype(BF16))
    k4, v4 = _shared_kv(x, kv_norm_g[None], w_kv.astype(BF16),
                        jnp.tile(k_norm_g, N_KV_HEADS)[None])
    for l in range(n_b):
        x = _swa_layer(x, k4, v4, b_norm_g[l][None], b_w_in[l].astype(BF16),
                       jnp.tile(q_norm_g[l], 2)[None], sinks[l], b_w_out[l].astype(BF16))
    return x
```

```python
import functools

import jax
import jax.numpy as jnp
from jax import lax
from jax.experimental import pallas as pl
from jax.experimental.pallas import tpu as pltpu

D_MODEL = 1024
LRU_WIDTH = 1536
LRU_BLOCKS = 16
LRU_BLOCK_W = LRU_WIDTH // LRU_BLOCKS
CONV_WIDTH = 4
LRU_C = 8.0
N_Q_HEADS = 16
N_KV_HEADS = 2
GROUP = N_Q_HEADS // N_KV_HEADS
HEAD_DIM = 64
ATT_WIDTH = N_Q_HEADS * HEAD_DIM
WINDOW = 128
BLOCK = 128
EPS = 1e-6

SUBLANES = 8
LANES = 128
STREAM_LEN = BLOCK // SUBLANES
CONV_TAIL_ROWS = (CONV_WIDTH - 1) * SUBLANES
GATE_GROUP_BLOCKS = 4
GATE_GROUP_W = GATE_GROUP_BLOCKS * LRU_BLOCK_W
N_GATE_GROUPS = LRU_BLOCKS // GATE_GROUP_BLOCKS
PAIRS_PER_KV = GROUP // 2
KV_LAYOUT_W = 2 * N_KV_HEADS * LANES

RGLRU_CHUNK = 512
RGLRU_SUB_ROWS = 256
MXU_PIECE_W = 512
SWA_CHUNK = 256
VMEM_LIMIT_BYTES = 56 * 1024 * 1024
F32_MIN_NORMAL = 1.17549435e-38

F32 = jnp.float32
BF16 = jnp.bfloat16


def _rmsnorm_rows(x, g):
    ms = jnp.mean(x * x, axis=-1, keepdims=True)
    return x * lax.rsqrt(ms + EPS) * g


def _softplus(z):
    return jnp.maximum(z, 0.0) + jnp.log1p(jnp.exp(-jnp.abs(z)))


def _load_permuted(x_ref, slab_ref, dst_ref):
    _, t, d = x_ref.shape
    for c in range(d // LANES):
        slab_ref[c] = x_ref[0, :, c * LANES:(c + 1) * LANES]
    for c in range(d // LANES):
        for n in range(t // BLOCK):
            for j in range(STREAM_LEN):
                r0 = n * BLOCK + j * SUBLANES
                dst_ref[r0:r0 + SUBLANES, c * LANES:(c + 1) * LANES] = (
                    slab_ref[c, pl.ds(n * BLOCK + j, SUBLANES, stride=STREAM_LEN), :])


def _store_unpermuted(o_ref, slab_ref, val):
    _, t, d = o_ref.shape
    for c in range(d // LANES):
        for n in range(t // BLOCK):
            for j in range(STREAM_LEN):
                r0 = n * BLOCK + j * SUBLANES
                slab_ref[c, pl.ds(n * BLOCK + j, SUBLANES, stride=STREAM_LEN), :] = (
                    val[r0:r0 + SUBLANES, c * LANES:(c + 1) * LANES])
    for c in range(d // LANES):
        o_ref[0, :, c * LANES:(c + 1) * LANES] = slab_ref[c]


def _pair_layout(t):
    lane = lax.broadcasted_iota(jnp.int32, t.shape, 1)
    low = lane < HEAD_DIM
    swapped = pltpu.roll(t, HEAD_DIM, axis=1)
    zero = jnp.zeros_like(t)
    return jnp.concatenate([
        jnp.where(low, t, zero), jnp.where(low, zero, swapped),
        jnp.where(low, swapped, zero), jnp.where(low, zero, t)], axis=1)


def _shared_kv(x, kvg_ref, wkv_ref, kg_ref):
    h = _rmsnorm_rows(x, kvg_ref[...]).astype(BF16)
    kv = jnp.dot(h, wkv_ref[...], preferred_element_type=F32)
    kw = N_KV_HEADS * HEAD_DIM
    k = kv[:, :kw]
    v = kv[:, kw:]
    low = lax.broadcasted_iota(jnp.int32, k.shape, 1) < HEAD_DIM
    k2 = k * k
    s_low = jnp.sum(jnp.where(low, k2, 0.0), axis=-1, keepdims=True)
    s_all = jnp.sum(k2, axis=-1, keepdims=True)
    ms = jnp.where(low, s_low, s_all - s_low) * (1.0 / HEAD_DIM)
    kn = k * lax.rsqrt(ms + EPS) * kg_ref[...] * (HEAD_DIM ** -0.5)
    return _pair_layout(kn).astype(BF16), _pair_layout(v).astype(BF16)


def _recurrence_unit(u_ref, y_ref, tail_ref, hcar_ref, convw_ref, convb_ref, wg_ref, brh_ref,
                     bih_ref, half_scale, row0, grp):
    w = GATE_GROUP_W
    lo = grp * w
    hi = lo + w
    rows = lambda j: slice(row0 + j * SUBLANES, row0 + (j + 1) * SUBLANES)
    x_row = lambda j: u_ref[rows(j), lo:hi]

    n_tail = CONV_WIDTH - 1
    prev_tail = tail_ref[:, lo:hi].reshape(n_tail, SUBLANES, w)
    cur_tail = jnp.concatenate(
        [x_row(STREAM_LEN - n_tail + i) for i in range(n_tail)], axis=0).reshape(
            n_tail, SUBLANES, w)
    sub3 = lax.broadcasted_iota(jnp.int32, (n_tail, SUBLANES, w), 1)
    wrapped = pltpu.roll(jnp.where(sub3 == SUBLANES - 1, prev_tail, cur_tail), 1, axis=1)
    tail_ref[:, lo:hi] = cur_tail.reshape(CONV_TAIL_ROWS, w)
    window = [wrapped[i] for i in range(n_tail)]
    taps = [convw_ref[k:k + 1, lo:hi] for k in range(CONV_WIDTH)]
    conv_b = convb_ref[:, lo:hi]
    xh = []
    for j in range(STREAM_LEN):
        xj = x_row(j)
        acc = conv_b + taps[n_tail] * xj
        for k in range(n_tail):
            acc = acc + taps[k] * window[k]
        window = window[1:] + [xj]
        xh.append(acc)

    zh = jnp.dot(jnp.concatenate(xh, axis=0).astype(BF16), wg_ref[grp],
                 preferred_element_type=F32)
    brh = brh_ref[:, lo:hi]
    bih = bih_ref[:, lo:hi]
    hs_c = half_scale[:, lo:hi]
    a_rows = []
    b_rows = []
    a_acc = h_acc = None
    for j in range(STREAM_LEN):
        r = slice(j * SUBLANES, (j + 1) * SUBLANES)
        t_r = jnp.tanh(zh[r, :w] + brh)
        t_i = jnp.tanh(zh[r, w:] + bih)
        log_a = hs_c * t_r + hs_c
        a = jnp.exp(log_a)
        one_m_a2 = (-1.0 - a * a) * jnp.tanh(log_a)
        root = one_m_a2 * lax.rsqrt(jnp.maximum(one_m_a2, F32_MIN_NORMAL))
        b = (root * xh[j]) * (1.0 + t_i)
        if j == 0:
            a_acc, h_acc = a, b
        else:
            h_acc = a * h_acc + b
            a_acc = a * a_acc
        a_rows.append(a)
        b_rows.append(b)

    sub = lax.broadcasted_iota(jnp.int32, (SUBLANES, w), 0)
    shift = 1
    while shift < SUBLANES:
        keep = sub >= shift
        a_prev = jnp.where(keep, pltpu.roll(a_acc, shift, axis=0), 1.0)
        h_prev = jnp.where(keep, pltpu.roll(h_acc, shift, axis=0), 0.0)
        h_acc = a_acc * h_prev + h_acc
        a_acc = a_acc * a_prev
        shift *= 2
    h_in = hcar_ref[:, lo:hi]
    ends = a_acc * h_in + h_acc
    hcar_ref[:, lo:hi] = ends[SUBLANES - 1:SUBLANES]
    h = jnp.where(sub == 0, h_in, pltpu.roll(ends, 1, axis=0))

    ys = []
    for j in range(STREAM_LEN):
        h = a_rows[j] * h + b_rows[j]
        gh = u_ref[rows(j), LRU_WIDTH + lo:LRU_WIDTH + hi]
        ys.append(h * (gh * (1.0 + jnp.tanh(gh))))
    y_ref[row0:row0 + BLOCK, lo:hi] = jnp.concatenate(ys, axis=0).astype(BF16)


def _interleave(units, pieces):
    slots = [[] for _ in units]
    for k, piece in enumerate(pieces):
        slots[(k * len(units)) // len(pieces)].append(piece)
    for unit, before in zip(units, slots):
        for piece in before:
            piece()
        unit()


def _rglru_kernel(*refs, permute_in, emit_kv):
    (x_ref, g_ref, win_ref, convw_ref, convb_ref, wg_ref, brh_ref, bih_ref, lam_ref,
     wout_ref) = refs[:10]
    refs = refs[10:]
    if emit_kv:
        kvg_ref, wkv_ref, kg_ref = refs[:3]
        refs = refs[3:]
    o_ref = refs[0]
    refs = refs[1:]
    if emit_kv:
        k_ref, v_ref = refs[:2]
        refs = refs[2:]
    tail_ref, hcar_ref, hb_ref, u_ref, y_ref = refs[:5]
    chunk = x_ref.shape[1]
    n_sub = chunk // RGLRU_SUB_ROWS

    @pl.when(pl.program_id(1) == 0)
    def _():
        tail_ref[...] = jnp.zeros_like(tail_ref)
        hcar_ref[...] = jnp.zeros_like(hcar_ref)

    if permute_in:
        slab_ref, xp_ref = refs[5:7]
        _load_permuted(x_ref, slab_ref, xp_ref)
        x_rows = lambda r0, n, cols=slice(None): xp_ref[r0:r0 + n, cols]
    else:
        x_rows = lambda r0, n, cols=slice(None): x_ref[0, r0:r0 + n, cols]

    half_scale = (-0.5 * LRU_C) * _softplus(-lam_ref[...])

    def in_proj_pieces(sb):
        r0 = sb * RGLRU_SUB_ROWS
        slot = sb % 2

        def piece(c):
            if c == 0:
                hb_ref[...] = _rmsnorm_rows(x_rows(r0, RGLRU_SUB_ROWS), g_ref[...]).astype(BF16)
            cols = slice(c * MXU_PIECE_W, (c + 1) * MXU_PIECE_W)
            u_ref[slot, :, cols] = jnp.dot(hb_ref[...], win_ref[:, cols],
                                           preferred_element_type=F32)
        return [functools.partial(piece, c) for c in range(2 * LRU_WIDTH // MXU_PIECE_W)]

    def recurrence_units(sb):
        slot = sb % 2
        return [functools.partial(
            _recurrence_unit, u_ref.at[slot], y_ref.at[slot], tail_ref, hcar_ref, convw_ref,
            convb_ref, wg_ref, brh_ref, bih_ref, half_scale, n * BLOCK, grp)
            for n in range(RGLRU_SUB_ROWS // BLOCK) for grp in range(N_GATE_GROUPS)]

    def out_proj_pieces(sb):
        r0 = sb * RGLRU_SUB_ROWS
        slot = sb % 2

        def piece(c):
            cols = slice(c * MXU_PIECE_W, (c + 1) * MXU_PIECE_W)
            o_ref[0, r0:r0 + RGLRU_SUB_ROWS, cols] = (
                x_rows(r0, RGLRU_SUB_ROWS, cols)
                + jnp.dot(y_ref[slot], wout_ref[:, cols], preferred_element_type=F32))

        def kv_piece():
            k4, v4 = _shared_kv(o_ref[0, r0:r0 + RGLRU_SUB_ROWS, :], kvg_ref, wkv_ref, kg_ref)
            k_ref[0, r0:r0 + RGLRU_SUB_ROWS, :] = k4
            v_ref[0, r0:r0 + RGLRU_SUB_ROWS, :] = v4

        pieces = [functools.partial(piece, c) for c in range(D_MODEL // MXU_PIECE_W)]
        return pieces + ([kv_piece] if emit_kv else [])

    for piece in in_proj_pieces(0):
        piece()
    for sb in range(n_sub):
        pieces = in_proj_pieces(sb + 1) if sb + 1 < n_sub else []
        if sb > 0:
            pieces = pieces + out_proj_pieces(sb - 1)
        _interleave(recurrence_units(sb), pieces)
    for piece in out_proj_pieces(n_sub - 1):
        piece()


def _const_spec(shape):
    zeros = (0,) * len(shape)
    return pl.BlockSpec(shape, lambda *_: zeros, pipeline_mode=pl.Buffered(1))


def _rglru_layer(x, norm_g, w_in, conv_w, conv_b, w_gates, brh, bih, lam, w_out, kv_params,
                 *, permute_in):
    bsz, seq, d = x.shape
    chunk = RGLRU_CHUNK
    emit_kv = kv_params is not None
    const = lambda *shape: _const_spec(shape)
    tile = lambda w: pl.BlockSpec((1, chunk, w), lambda b, t: (b, t, 0))
    in_specs = [
        tile(d),
        const(1, d),
        const(d, 2 * LRU_WIDTH),
        const(CONV_WIDTH, LRU_WIDTH),
        const(1, LRU_WIDTH),
        const(N_GATE_GROUPS, GATE_GROUP_W, 2 * GATE_GROUP_W),
        const(1, LRU_WIDTH),
        const(1, LRU_WIDTH),
        const(1, LRU_WIDTH),
        const(LRU_WIDTH, d),
    ]
    args = [x, norm_g, w_in, conv_w, conv_b, w_gates, brh, bih, lam, w_out]
    out_specs = [tile(d)]
    out_shape = [jax.ShapeDtypeStruct(x.shape, x.dtype)]
    if emit_kv:
        kw = N_KV_HEADS * HEAD_DIM
        in_specs += [const(1, d), const(d, 2 * kw), const(1, kw)]
        args += list(kv_params)
        out_specs += [tile(KV_LAYOUT_W)] * 2
        out_shape += [jax.ShapeDtypeStruct((bsz, seq, KV_LAYOUT_W), BF16)] * 2
    scratch = [
        pltpu.VMEM((CONV_TAIL_ROWS, LRU_WIDTH), F32),
        pltpu.VMEM((1, LRU_WIDTH), F32),
        pltpu.VMEM((RGLRU_SUB_ROWS, d), BF16),
        pltpu.VMEM((2, RGLRU_SUB_ROWS, 2 * LRU_WIDTH), F32),
        pltpu.VMEM((2, RGLRU_SUB_ROWS, LRU_WIDTH), BF16),
    ]
    if permute_in:
        scratch += [pltpu.VMEM((d // LANES, chunk, LANES), F32), pltpu.VMEM((chunk, d), F32)]
    return pl.pallas_call(
        functools.partial(_rglru_kernel, permute_in=permute_in, emit_kv=emit_kv),
        grid=(bsz, seq // chunk),
        in_specs=in_specs,
        out_specs=out_specs,
        out_shape=out_shape,
        scratch_shapes=scratch,
        compiler_params=pltpu.CompilerParams(
            dimension_semantics=("arbitrary", "arbitrary"),
            vmem_limit_bytes=VMEM_LIMIT_BYTES),
        name="rglru_kv_layer" if emit_kv else "rglru_layer",
    )(*args)


def _swa_kernel(sinks_ref, x_ref, g_ref, win_ref, qg_ref, kprev_ref, kcur_ref, vprev_ref,
                vcur_ref, wout_ref, o_ref, *scratch, unpermute_out):
    chunk = x_ref.shape[1]
    first_chunk = pl.program_id(1) == 0

    x = x_ref[0]
    h = _rmsnorm_rows(x, g_ref[...]).astype(BF16)
    u = jnp.dot(h, win_ref[...], preferred_element_type=F32)
    gh = u[:, ATT_WIDTH:]

    lane = lax.broadcasted_iota(jnp.int32, (chunk, LANES), 1)
    low = lane < HEAD_DIM
    qn = []
    for p in range(N_Q_HEADS // 2):
        q = u[:, p * LANES:(p + 1) * LANES]
        q2 = q * q
        s_low = jnp.sum(jnp.where(low, q2, 0.0), axis=-1, keepdims=True)
        s_all = jnp.sum(q2, axis=-1, keepdims=True)
        ms = jnp.where(low, s_low, s_all - s_low) * (1.0 / HEAD_DIM)
        qn.append((q * lax.rsqrt(ms + EPS) * qg_ref[...]).astype(BF16))

    ri = lax.broadcasted_iota(jnp.int32, (BLOCK, BLOCK), 0)
    ci = lax.broadcasted_iota(jnp.int32, (BLOCK, BLOCK), 1)
    q_time = (ri & (SUBLANES - 1)) * STREAM_LEN + (ri >> 3)
    k_time = (ci & (SUBLANES - 1)) * STREAM_LEN + (ci >> 3)
    from_prev = k_time > q_time
    dist = ((q_time - k_time) & (BLOCK - 1)).astype(F32)
    low_o = lax.broadcasted_iota(jnp.int32, (BLOCK, LANES), 1) < HEAD_DIM

    o_blocks = []
    for n in range(chunk // BLOCK):
        rows = slice(n * BLOCK, (n + 1) * BLOCK)
        o_pairs = []
        for j in range(N_KV_HEADS):
            q_stack = jnp.concatenate(
                [qn[j * PAIRS_PER_KV + i][rows] for i in range(PAIRS_PER_KV)], axis=0)
            p_stack = []
            inv_den = []
            for e in range(2):
                cols = slice((2 * j + e) * LANES, (2 * j + e + 1) * LANES)
                if n == 0:
                    k_band = jnp.concatenate([kprev_ref[0, :, cols], kcur_ref[0, 0:BLOCK, cols]],
                                             axis=0)
                else:
                    k_band = kcur_ref[0, (n - 1) * BLOCK:(n + 1) * BLOCK, cols]
                s = lax.dot_general(q_stack, k_band, (((1,), (1,)), ((), ())),
                                    preferred_element_type=F32)
                p_e = []
                inv_e = []
                for i in range(PAIRS_PER_KV):
                    head = j * GROUP + 2 * i + e
                    slope = 2.0 ** (-8.0 * (head + 1) / N_Q_HEADS)
                    s_h = s[i * BLOCK:(i + 1) * BLOCK]
                    sc = jnp.where(from_prev, s_h[:, :BLOCK], s_h[:, BLOCK:]) - slope * dist
                    if n == 0:
                        sc = jnp.where(jnp.logical_and(from_prev, first_chunk), -jnp.inf, sc)
                    sink = sinks_ref[head]
                    m = jnp.maximum(jnp.max(sc, axis=-1, keepdims=True), sink)
                    pr = jnp.exp(sc - m)
                    den = jnp.sum(pr, axis=-1, keepdims=True) + jnp.exp(sink - m)
                    inv_e.append(1.0 / den)
                    zero = jnp.zeros_like(pr)
                    p_e.append(jnp.concatenate(
                        [jnp.where(from_prev, pr, zero), jnp.where(from_prev, zero, pr)],
                        axis=1).astype(BF16))
                p_stack.append(jnp.concatenate(p_e, axis=0))
                inv_den.append(inv_e)
            acc = None
            for e in range(2):
                cols = slice((2 * j + e) * LANES, (2 * j + e + 1) * LANES)
                if n == 0:
                    v_band = jnp.concatenate([vprev_ref[0, :, cols], vcur_ref[0, 0:BLOCK, cols]],
                                             axis=0)
                else:
                    v_band = vcur_ref[0, (n - 1) * BLOCK:(n + 1) * BLOCK, cols]
                part = jnp.dot(p_stack[e], v_band, preferred_element_type=F32)
                acc = part if acc is None else acc + part
            for i in range(PAIRS_PER_KV):
                inv = jnp.where(low_o, inv_den[0][i], inv_den[1][i])
                o_pairs.append(acc[i * BLOCK:(i + 1) * BLOCK] * inv)
        o_blocks.append(jnp.concatenate(o_pairs, axis=1))
    o = jnp.concatenate(o_blocks, axis=0)
    y = (o * (gh * (1.0 + jnp.tanh(gh)))).astype(BF16)
    out = x + jnp.dot(y, wout_ref[...], preferred_element_type=F32)
    if unpermute_out:
        _store_unpermuted(o_ref, scratch[0], out)
    else:
        o_ref[0] = out


def _swa_layer(x, k4, v4, norm_g, w_in, q_norm_g2, sinks, w_out, *, unpermute_out):
    bsz, seq, d = x.shape
    chunk = SWA_CHUNK
    blocks_per_chunk = chunk // BLOCK
    const = lambda *shape: _const_spec(shape)
    prev_spec = pl.BlockSpec(
        (1, BLOCK, KV_LAYOUT_W),
        lambda b, t, s: (b, jnp.maximum(t * blocks_per_chunk - 1, 0), 0))
    cur_spec = pl.BlockSpec((1, chunk, KV_LAYOUT_W), lambda b, t, s: (b, t, 0))
    grid_spec = pltpu.PrefetchScalarGridSpec(
        num_scalar_prefetch=1,
        grid=(bsz, seq // chunk),
        in_specs=[
            pl.BlockSpec((1, chunk, d), lambda b, t, s: (b, t, 0)),
            const(1, d),
            const(d, 2 * ATT_WIDTH),
            const(1, LANES),
            prev_spec, cur_spec, prev_spec, cur_spec,
            const(ATT_WIDTH, d),
        ],
        out_specs=pl.BlockSpec((1, chunk, d), lambda b, t, s: (b, t, 0)),
        scratch_shapes=(
            [pltpu.VMEM((d // LANES, chunk, LANES), F32)] if unpermute_out else []),
    )
    return pl.pallas_call(
        functools.partial(_swa_kernel, unpermute_out=unpermute_out),
        grid_spec=grid_spec,
        out_shape=jax.ShapeDtypeStruct(x.shape, x.dtype),
        compiler_params=pltpu.CompilerParams(
            dimension_semantics=("arbitrary", "arbitrary"),
            vmem_limit_bytes=VMEM_LIMIT_BYTES),
        name="swa_out_layer" if unpermute_out else "swa_layer",
    )(sinks, x, norm_g, w_in, q_norm_g2, k4, k4, v4, v4, w_out)


def _gate_group_weights(wr, wi):
    def blockdiag(w):
        w = w.reshape(N_GATE_GROUPS, GATE_GROUP_BLOCKS, LRU_BLOCK_W, LRU_BLOCK_W)
        eye = jnp.eye(GATE_GROUP_BLOCKS, dtype=w.dtype)
        full = w[:, :, :, None, :] * eye[None, :, None, :, None]
        return full.reshape(N_GATE_GROUPS, GATE_GROUP_W, GATE_GROUP_W)
    return jnp.concatenate([blockdiag(wr), blockdiag(wi)], axis=-1).astype(BF16)


def _halve_gate_columns(w, split):
    return jnp.concatenate([w[:, :split], 0.5 * w[:, split:]], axis=1).astype(BF16)


def kernel(x, a_norm_g, a_w_in, a_conv_w, a_conv_b, a_gate_r_w, a_gate_r_b, a_gate_i_w,
           a_gate_i_b, a_lambda, a_w_out, kv_norm_g, w_kv, k_norm_g, b_norm_g, b_w_in,
           q_norm_g, sinks, b_w_out):
    n_a = a_w_in.shape[0]
    n_b = b_w_in.shape[0]
    kv_params = (kv_norm_g[None], w_kv.astype(BF16), jnp.tile(k_norm_g, N_KV_HEADS)[None])
    k4 = v4 = None
    for l in range(n_a):
        last = l == n_a - 1
        res = _rglru_layer(
            x, a_norm_g[l][None], _halve_gate_columns(a_w_in[l], LRU_WIDTH),
            0.5 * a_conv_w[l], 0.5 * a_conv_b[l][None],
            _gate_group_weights(a_gate_r_w[l], a_gate_i_w[l]),
            0.5 * a_gate_r_b[l][None], 0.5 * a_gate_i_b[l][None], a_lambda[l][None],
            a_w_out[l].astype(BF16), kv_params if last else None, permute_in=(l == 0))
        if last:
            x, k4, v4 = res
        else:
            x = res[0]
    for l in range(n_b):
        x = _swa_layer(x, k4, v4, b_norm_g[l][None], _halve_gate_columns(b_w_in[l], ATT_WIDTH),
                       jnp.tile(q_norm_g[l], 2)[None], sinks[l], b_w_out[l].astype(BF16),
                       unpermute_out=(l == n_b - 1))
    return x
```

```python
import functools

import jax
import jax.numpy as jnp
from jax import lax
from jax.experimental import pallas as pl
from jax.experimental.pallas import tpu as pltpu

D_MODEL = 1024
LRU_WIDTH = 1536
LRU_BLOCKS = 16
LRU_BLOCK_W = LRU_WIDTH // LRU_BLOCKS
CONV_WIDTH = 4
LRU_C = 8.0
N_Q_HEADS = 16
N_KV_HEADS = 2
GROUP = N_Q_HEADS // N_KV_HEADS
HEAD_DIM = 64
ATT_WIDTH = N_Q_HEADS * HEAD_DIM
WINDOW = 128
BLOCK = 128
EPS = 1e-6

SUBLANES = 8
LANES = 128
STREAM_LEN = BLOCK // SUBLANES
CONV_TAIL_ROWS = (CONV_WIDTH - 1) * SUBLANES
GATE_GROUP_BLOCKS = 4
GATE_GROUP_W = GATE_GROUP_BLOCKS * LRU_BLOCK_W
N_GATE_GROUPS = LRU_BLOCKS // GATE_GROUP_BLOCKS
PAIRS_PER_KV = GROUP // 2
KV_LAYOUT_W = 2 * N_KV_HEADS * LANES

RGLRU_CHUNK = 1024
SWA_CHUNK = 512
SUB_ROWS = 256
MXU_PIECE_W = 512
VMEM_LIMIT_BYTES = 56 * 1024 * 1024
F32_MIN_NORMAL = 1.17549435e-38

F32 = jnp.float32
BF16 = jnp.bfloat16


def _rmsnorm_rows(x, g):
    ms = jnp.mean(x * x, axis=-1, keepdims=True)
    return x * lax.rsqrt(ms + EPS) * g


def _softplus(z):
    return jnp.maximum(z, 0.0) + jnp.log1p(jnp.exp(-jnp.abs(z)))


def _load_permuted(x_ref, slab_ref, dst_ref):
    _, t, d = x_ref.shape
    for c in range(d // LANES):
        slab_ref[c] = x_ref[0, :, c * LANES:(c + 1) * LANES]
    for c in range(d // LANES):
        for n in range(t // BLOCK):
            for j in range(STREAM_LEN):
                r0 = n * BLOCK + j * SUBLANES
                dst_ref[r0:r0 + SUBLANES, c * LANES:(c + 1) * LANES] = (
                    slab_ref[c, pl.ds(n * BLOCK + j, SUBLANES, stride=STREAM_LEN), :])


def _scatter_unpermuted(slab_ref, val, row0, lane_tile0):
    rows, w = val.shape
    for c in range(w // LANES):
        for n in range(rows // BLOCK):
            for j in range(STREAM_LEN):
                r0 = n * BLOCK + j * SUBLANES
                slab_ref[lane_tile0 + c,
                         pl.ds(row0 + n * BLOCK + j, SUBLANES, stride=STREAM_LEN), :] = (
                    val[r0:r0 + SUBLANES, c * LANES:(c + 1) * LANES])


def _pair_layout(t):
    lane = lax.broadcasted_iota(jnp.int32, t.shape, 1)
    low = lane < HEAD_DIM
    swapped = pltpu.roll(t, HEAD_DIM, axis=1)
    zero = jnp.zeros_like(t)
    return jnp.concatenate([
        jnp.where(low, t, zero), jnp.where(low, zero, swapped),
        jnp.where(low, swapped, zero), jnp.where(low, zero, t)], axis=1)


def _shared_kv(x, kvg_ref, wkv_ref, kg_ref):
    h = _rmsnorm_rows(x, kvg_ref[...]).astype(BF16)
    kv = jnp.dot(h, wkv_ref[...], preferred_element_type=F32)
    kw = N_KV_HEADS * HEAD_DIM
    k = kv[:, :kw]
    v = kv[:, kw:]
    low = lax.broadcasted_iota(jnp.int32, k.shape, 1) < HEAD_DIM
    k2 = k * k
    s_low = jnp.sum(jnp.where(low, k2, 0.0), axis=-1, keepdims=True)
    s_all = jnp.sum(k2, axis=-1, keepdims=True)
    ms = jnp.where(low, s_low, s_all - s_low) * (1.0 / HEAD_DIM)
    kn = k * lax.rsqrt(ms + EPS) * kg_ref[...] * (HEAD_DIM ** -0.5)
    return _pair_layout(kn).astype(BF16), _pair_layout(v).astype(BF16)


def _recurrence_unit(u_ref, y_ref, tail_ref, hcar_ref, convw_ref, convb_ref, wg_ref, brh_ref,
                     bih_ref, half_scale, row0, grp):
    w = GATE_GROUP_W
    lo = grp * w
    hi = lo + w
    rows = lambda j: slice(row0 + j * SUBLANES, row0 + (j + 1) * SUBLANES)
    x_row = lambda j: u_ref[rows(j), lo:hi]

    n_tail = CONV_WIDTH - 1
    prev_tail = tail_ref[:, lo:hi].reshape(n_tail, SUBLANES, w)
    cur_tail = jnp.concatenate(
        [x_row(STREAM_LEN - n_tail + i) for i in range(n_tail)], axis=0).reshape(
            n_tail, SUBLANES, w)
    sub3 = lax.broadcasted_iota(jnp.int32, (n_tail, SUBLANES, w), 1)
    wrapped = pltpu.roll(jnp.where(sub3 == SUBLANES - 1, prev_tail, cur_tail), 1, axis=1)
    tail_ref[:, lo:hi] = cur_tail.reshape(CONV_TAIL_ROWS, w)
    window = [wrapped[i] for i in range(n_tail)]
    taps = [convw_ref[k:k + 1, lo:hi] for k in range(CONV_WIDTH)]
    conv_b = convb_ref[:, lo:hi]
    xh = []
    for j in range(STREAM_LEN):
        xj = x_row(j)
        acc = conv_b + taps[n_tail] * xj
        for k in range(n_tail):
            acc = acc + taps[k] * window[k]
        window = window[1:] + [xj]
        xh.append(acc)

    zh = jnp.dot(jnp.concatenate(xh, axis=0).astype(BF16), wg_ref[grp],
                 preferred_element_type=F32)
    brh = brh_ref[:, lo:hi]
    bih = bih_ref[:, lo:hi]
    hs_c = half_scale[:, lo:hi]
    a_rows = []
    b_rows = []
    a_acc = h_acc = None
    for j in range(STREAM_LEN):
        r = slice(j * SUBLANES, (j + 1) * SUBLANES)
        t_r = jnp.tanh(zh[r, :w] + brh)
        t_i = jnp.tanh(zh[r, w:] + bih)
        log_a = hs_c * t_r + hs_c
        a = jnp.exp(log_a)
        one_m_a2 = (-1.0 - a * a) * jnp.tanh(log_a)
        root = one_m_a2 * lax.rsqrt(jnp.maximum(one_m_a2, F32_MIN_NORMAL))
        b = (root * xh[j]) * (1.0 + t_i)
        if j == 0:
            a_acc, h_acc = a, b
        else:
            h_acc = a * h_acc + b
            a_acc = a * a_acc
        a_rows.append(a)
        b_rows.append(b)

    sub = lax.broadcasted_iota(jnp.int32, (SUBLANES, w), 0)
    shift = 1
    while shift < SUBLANES:
        keep = sub >= shift
        a_prev = jnp.where(keep, pltpu.roll(a_acc, shift, axis=0), 1.0)
        h_prev = jnp.where(keep, pltpu.roll(h_acc, shift, axis=0), 0.0)
        h_acc = a_acc * h_prev + h_acc
        a_acc = a_acc * a_prev
        shift *= 2
    h_in = hcar_ref[:, lo:hi]
    ends = a_acc * h_in + h_acc
    hcar_ref[:, lo:hi] = ends[SUBLANES - 1:SUBLANES]
    h = jnp.where(sub == 0, h_in, pltpu.roll(ends, 1, axis=0))

    ys = []
    for j in range(STREAM_LEN):
        h = a_rows[j] * h + b_rows[j]
        gh = u_ref[rows(j), LRU_WIDTH + lo:LRU_WIDTH + hi]
        ys.append(h * (gh * (1.0 + jnp.tanh(gh))))
    y_ref[row0:row0 + BLOCK, lo:hi] = jnp.concatenate(ys, axis=0).astype(BF16)


def _interleave(units, pieces):
    slots = [[] for _ in units]
    for k, piece in enumerate(pieces):
        slots[(k * len(units)) // len(pieces)].append(piece)
    for unit, before in zip(units, slots):
        for piece in before:
            piece()
        unit()


def _run_pipeline(n_sub, in_pieces, mixer_units, out_pieces):
    for piece in in_pieces(0):
        piece()
    for sb in range(n_sub):
        pieces = in_pieces(sb + 1) if sb + 1 < n_sub else []
        if sb > 0:
            pieces = pieces + out_pieces(sb - 1)
        _interleave(mixer_units(sb), pieces)
    for piece in out_pieces(n_sub - 1):
        piece()


def _rglru_kernel(*refs, permute_in, emit_kv):
    (x_ref, g_ref, win_ref, convw_ref, convb_ref, wg_ref, brh_ref, bih_ref, lam_ref,
     wout_ref) = refs[:10]
    refs = refs[10:]
    if emit_kv:
        kvg_ref, wkv_ref, kg_ref = refs[:3]
        refs = refs[3:]
    o_ref = refs[0]
    refs = refs[1:]
    if emit_kv:
        k_ref, v_ref = refs[:2]
        refs = refs[2:]
    tail_ref, hcar_ref, hb_ref, u_ref, y_ref = refs[:5]
    chunk = x_ref.shape[1]

    @pl.when(pl.program_id(1) == 0)
    def _():
        tail_ref[...] = jnp.zeros_like(tail_ref)
        hcar_ref[...] = jnp.zeros_like(hcar_ref)

    if permute_in:
        slab_ref, xp_ref = refs[5:7]
        _load_permuted(x_ref, slab_ref, xp_ref)
        x_rows = lambda r0, cols=slice(None): xp_ref[r0:r0 + SUB_ROWS, cols]
    else:
        x_rows = lambda r0, cols=slice(None): x_ref[0, r0:r0 + SUB_ROWS, cols]

    half_scale = (-0.5 * LRU_C) * _softplus(-lam_ref[...])

    def in_proj_pieces(sb):
        r0 = sb * SUB_ROWS
        slot = sb % 2

        def piece(c):
            if c == 0:
                hb_ref[...] = _rmsnorm_rows(x_rows(r0), g_ref[...]).astype(BF16)
            cols = slice(c * MXU_PIECE_W, (c + 1) * MXU_PIECE_W)
            u_ref[slot, :, cols] = jnp.dot(hb_ref[...], win_ref[:, cols],
                                           preferred_element_type=F32)
        return [functools.partial(piece, c) for c in range(2 * LRU_WIDTH // MXU_PIECE_W)]

    def recurrence_units(sb):
        slot = sb % 2
        return [functools.partial(
            _recurrence_unit, u_ref.at[slot], y_ref.at[slot], tail_ref, hcar_ref, convw_ref,
            convb_ref, wg_ref, brh_ref, bih_ref, half_scale, n * BLOCK, grp)
            for n in range(SUB_ROWS // BLOCK) for grp in range(N_GATE_GROUPS)]

    def out_proj_pieces(sb):
        r0 = sb * SUB_ROWS
        slot = sb % 2

        def piece(c):
            cols = slice(c * MXU_PIECE_W, (c + 1) * MXU_PIECE_W)
            o_ref[0, r0:r0 + SUB_ROWS, cols] = (
                x_rows(r0, cols)
                + jnp.dot(y_ref[slot], wout_ref[:, cols], preferred_element_type=F32))

        def kv_piece():
            k4, v4 = _shared_kv(o_ref[0, r0:r0 + SUB_ROWS, :], kvg_ref, wkv_ref, kg_ref)
            k_ref[0, r0:r0 + SUB_ROWS, :] = k4
            v_ref[0, r0:r0 + SUB_ROWS, :] = v4

        pieces = [functools.partial(piece, c) for c in range(D_MODEL // MXU_PIECE_W)]
        return pieces + ([kv_piece] if emit_kv else [])

    _run_pipeline(chunk // SUB_ROWS, in_proj_pieces, recurrence_units, out_proj_pieces)


def _layer_spec(shape, layer):
    zeros = (0,) * len(shape)
    return pl.BlockSpec((None,) + tuple(shape), lambda *_: (layer,) + zeros,
                        pipeline_mode=pl.Buffered(1))


def _const_spec(shape):
    zeros = (0,) * len(shape)
    return pl.BlockSpec(tuple(shape), lambda *_: zeros, pipeline_mode=pl.Buffered(1))


def _rglru_layer(x, layer, params, kv_params, *, permute_in):
    bsz, seq, d = x.shape
    chunk = RGLRU_CHUNK
    emit_kv = kv_params is not None
    tile = lambda w: pl.BlockSpec((1, chunk, w), lambda b, t: (b, t, 0))
    in_specs = [tile(d)] + [_layer_spec(p.shape[1:], layer) for p in params]
    args = [x] + list(params)
    out_specs = [tile(d)]
    out_shape = [jax.ShapeDtypeStruct(x.shape, x.dtype)]
    if emit_kv:
        in_specs += [_const_spec(p.shape) for p in kv_params]
        args += list(kv_params)
        out_specs += [tile(KV_LAYOUT_W)] * 2
        out_shape += [jax.ShapeDtypeStruct((bsz, seq, KV_LAYOUT_W), BF16)] * 2
    scratch = [
        pltpu.VMEM((CONV_TAIL_ROWS, LRU_WIDTH), F32),
        pltpu.VMEM((1, LRU_WIDTH), F32),
        pltpu.VMEM((SUB_ROWS, d), BF16),
        pltpu.VMEM((2, SUB_ROWS, 2 * LRU_WIDTH), F32),
        pltpu.VMEM((2, SUB_ROWS, LRU_WIDTH), BF16),
    ]
    if permute_in:
        scratch += [pltpu.VMEM((d // LANES, chunk, LANES), F32), pltpu.VMEM((chunk, d), F32)]
    return pl.pallas_call(
        functools.partial(_rglru_kernel, permute_in=permute_in, emit_kv=emit_kv),
        grid=(bsz, seq // chunk),
        in_specs=in_specs,
        out_specs=out_specs,
        out_shape=out_shape,
        scratch_shapes=scratch,
        compiler_params=pltpu.CompilerParams(
            dimension_semantics=("arbitrary", "arbitrary"),
            vmem_limit_bytes=VMEM_LIMIT_BYTES),
        name="rglru_kv_layer" if emit_kv else "rglru_layer",
    )(*args)


def _attention_unit(qn_ref, sg_ref, y_ref, kprev_ref, kcur_ref, vprev_ref, vcur_ref, sinks_ref,
                    sink_base, masks, first_chunk, row0, blk, j):
    from_prev, dist, low_o, ones = masks
    rows = slice(row0, row0 + BLOCK)
    pair_lanes = lambda i: slice((j * PAIRS_PER_KV + i) * LANES, (j * PAIRS_PER_KV + i + 1) * LANES)

    def band(prev_ref, cur_ref, cols):
        if blk == 0:
            return jnp.concatenate([prev_ref[0, :, cols], cur_ref[0, 0:BLOCK, cols]], axis=0)
        return cur_ref[0, (blk - 1) * BLOCK:(blk + 1) * BLOCK, cols]

    q_stack = jnp.concatenate([qn_ref[rows, pair_lanes(i)] for i in range(PAIRS_PER_KV)], axis=0)
    acc = None
    row_max = []
    for e in range(2):
        cols = slice((2 * j + e) * LANES, (2 * j + e + 1) * LANES)
        s = lax.dot_general(q_stack, band(kprev_ref, kcur_ref, cols), (((1,), (1,)), ((), ())),
                            preferred_element_type=F32)
        p_e = []
        m_e = []
        for i in range(PAIRS_PER_KV):
            head = j * GROUP + 2 * i + e
            slope = 2.0 ** (-8.0 * (head + 1) / N_Q_HEADS)
            s_h = s[i * BLOCK:(i + 1) * BLOCK]
            sc = jnp.where(from_prev, s_h[:, :BLOCK], s_h[:, BLOCK:]) - slope * dist
            if blk == 0:
                sc = jnp.where(jnp.logical_and(from_prev, first_chunk), -jnp.inf, sc)
            m = jnp.maximum(jnp.max(sc, axis=-1, keepdims=True), sinks_ref[sink_base + head])
            pr = jnp.exp(sc - m)
            zero = jnp.zeros_like(pr)
            p_e.append(jnp.concatenate(
                [jnp.where(from_prev, pr, zero), jnp.where(from_prev, zero, pr)],
                axis=1).astype(BF16))
            m_e.append(m)
        row_max.append(m_e)
        v_ext = jnp.concatenate([band(vprev_ref, vcur_ref, cols), ones[e]], axis=1)
        part = jnp.dot(jnp.concatenate(p_e, axis=0), v_ext, preferred_element_type=F32)
        acc = part if acc is None else acc + part
    for i in range(PAIRS_PER_KV):
        h_even = j * GROUP + 2 * i
        pair = acc[i * BLOCK:(i + 1) * BLOCK]
        m_pair = jnp.where(low_o, row_max[0][i], row_max[1][i])
        sink_pair = jnp.where(low_o, sinks_ref[sink_base + h_even],
                              sinks_ref[sink_base + h_even + 1])
        den = pair[:, LANES:] + jnp.exp(sink_pair - m_pair)
        o = pair[:, :LANES] * (1.0 / den)
        y_ref[rows, pair_lanes(i)] = (o * sg_ref[rows, pair_lanes(i)]).astype(BF16)


def _swa_kernel(sinks_ref, x_ref, g_ref, win_ref, qg_ref, kprev_ref, kcur_ref, vprev_ref,
                vcur_ref, wout_ref, o_ref, hb_ref, qn_ref, sg_ref, y_ref, *extra,
                sink_base, unpermute_out):
    chunk = x_ref.shape[1]
    first_chunk = pl.program_id(1) == 0

    ri = lax.broadcasted_iota(jnp.int32, (BLOCK, BLOCK), 0)
    ci = lax.broadcasted_iota(jnp.int32, (BLOCK, BLOCK), 1)
    q_time = (ri & (SUBLANES - 1)) * STREAM_LEN + (ri >> 3)
    k_time = (ci & (SUBLANES - 1)) * STREAM_LEN + (ci >> 3)
    from_prev = k_time > q_time
    dist = ((q_time - k_time) & (BLOCK - 1)).astype(F32)
    low_o = ci < HEAD_DIM
    low_kv = lax.broadcasted_iota(jnp.int32, (2 * BLOCK, LANES), 1) < HEAD_DIM
    ones = [jnp.where(low_kv, 1.0, 0.0).astype(BF16), jnp.where(low_kv, 0.0, 1.0).astype(BF16)]
    masks = (from_prev, dist, low_o, ones)
    low_q = lax.broadcasted_iota(jnp.int32, (SUB_ROWS, LANES), 1) < HEAD_DIM

    def in_proj_pieces(sb):
        r0 = sb * SUB_ROWS
        slot = sb % 2

        def piece(c):
            if c == 0:
                hb_ref[...] = _rmsnorm_rows(x_ref[0, r0:r0 + SUB_ROWS, :], g_ref[...]).astype(BF16)
            cols = slice(c * MXU_PIECE_W, (c + 1) * MXU_PIECE_W)
            u = jnp.dot(hb_ref[...], win_ref[:, cols], preferred_element_type=F32)
            if c < ATT_WIDTH // MXU_PIECE_W:
                for p in range(MXU_PIECE_W // LANES):
                    q = u[:, p * LANES:(p + 1) * LANES]
                    q2 = q * q
                    s_low = jnp.sum(jnp.where(low_q, q2, 0.0), axis=-1, keepdims=True)
                    s_all = jnp.sum(q2, axis=-1, keepdims=True)
                    ms = jnp.where(low_q, s_low, s_all - s_low) * (1.0 / HEAD_DIM)
                    lanes = slice(c * MXU_PIECE_W + p * LANES, c * MXU_PIECE_W + (p + 1) * LANES)
                    qn_ref[slot, :, lanes] = (q * lax.rsqrt(ms + EPS) * qg_ref[...]).astype(BF16)
            else:
                lo = c * MXU_PIECE_W - ATT_WIDTH
                sg_ref[slot, :, lo:lo + MXU_PIECE_W] = u * (1.0 + jnp.tanh(u))
        return [functools.partial(piece, c) for c in range(2 * ATT_WIDTH // MXU_PIECE_W)]

    def attention_units(sb):
        slot = sb % 2
        return [functools.partial(
            _attention_unit, qn_ref.at[slot], sg_ref.at[slot], y_ref.at[slot], kprev_ref,
            kcur_ref, vprev_ref, vcur_ref, sinks_ref, sink_base, masks, first_chunk,
            n * BLOCK, sb * (SUB_ROWS // BLOCK) + n, j)
            for n in range(SUB_ROWS // BLOCK) for j in range(N_KV_HEADS)]

    def out_proj_pieces(sb):
        r0 = sb * SUB_ROWS
        slot = sb % 2

        def piece(c):
            cols = slice(c * MXU_PIECE_W, (c + 1) * MXU_PIECE_W)
            out = (x_ref[0, r0:r0 + SUB_ROWS, cols]
                   + jnp.dot(y_ref[slot], wout_ref[:, cols], preferred_element_type=F32))
            if unpermute_out:
                _scatter_unpermuted(extra[0], out, r0, c * MXU_PIECE_W // LANES)
            else:
                o_ref[0, r0:r0 + SUB_ROWS, cols] = out
        return [functools.partial(piece, c) for c in range(D_MODEL // MXU_PIECE_W)]

    _run_pipeline(chunk // SUB_ROWS, in_proj_pieces, attention_units, out_proj_pieces)
    if unpermute_out:
        for c in range(D_MODEL // LANES):
            o_ref[0, :, c * LANES:(c + 1) * LANES] = extra[0][c]


def _swa_layer(x, k4, v4, layer, params, sinks_flat, *, unpermute_out):
    bsz, seq, d = x.shape
    chunk = SWA_CHUNK
    blocks_per_chunk = chunk // BLOCK
    norm_g, w_in, q_norm_g2, w_out = params
    prev_spec = pl.BlockSpec(
        (1, BLOCK, KV_LAYOUT_W),
        lambda b, t, s: (b, jnp.maximum(t * blocks_per_chunk - 1, 0), 0))
    cur_spec = pl.BlockSpec((1, chunk, KV_LAYOUT_W), lambda b, t, s: (b, t, 0))
    scratch = [
        pltpu.VMEM((SUB_ROWS, d), BF16),
        pltpu.VMEM((2, SUB_ROWS, ATT_WIDTH), BF16),
        pltpu.VMEM((2, SUB_ROWS, ATT_WIDTH), F32),
        pltpu.VMEM((2, SUB_ROWS, ATT_WIDTH), BF16),
    ]
    if unpermute_out:
        scratch.append(pltpu.VMEM((d // LANES, chunk, LANES), F32))
    grid_spec = pltpu.PrefetchScalarGridSpec(
        num_scalar_prefetch=1,
        grid=(bsz, seq // chunk),
        in_specs=[
            pl.BlockSpec((1, chunk, d), lambda b, t, s: (b, t, 0)),
            _layer_spec(norm_g.shape[1:], layer),
            _layer_spec(w_in.shape[1:], layer),
            _layer_spec(q_norm_g2.shape[1:], layer),
            prev_spec, cur_spec, prev_spec, cur_spec,
            _layer_spec(w_out.shape[1:], layer),
        ],
        out_specs=pl.BlockSpec((1, chunk, d), lambda b, t, s: (b, t, 0)),
        scratch_shapes=scratch,
    )
    return pl.pallas_call(
        functools.partial(_swa_kernel, sink_base=layer * N_Q_HEADS, unpermute_out=unpermute_out),
        grid_spec=grid_spec,
        out_shape=jax.ShapeDtypeStruct(x.shape, x.dtype),
        compiler_params=pltpu.CompilerParams(
            dimension_semantics=("arbitrary", "arbitrary"),
            vmem_limit_bytes=VMEM_LIMIT_BYTES),
        name="swa_out_layer" if unpermute_out else "swa_layer",
    )(sinks_flat, x, norm_g, w_in, q_norm_g2, k4, k4, v4, v4, w_out)


def _gate_group_weights(wr, wi):
    def blockdiag(w):
        n_l = w.shape[0]
        w = w.reshape(n_l, N_GATE_GROUPS, GATE_GROUP_BLOCKS, LRU_BLOCK_W, LRU_BLOCK_W)
        eye = jnp.eye(GATE_GROUP_BLOCKS, dtype=w.dtype)
        full = w[:, :, :, :, None, :] * eye[None, None, :, None, :, None]
        return full.reshape(n_l, N_GATE_GROUPS, GATE_GROUP_W, GATE_GROUP_W)
    return jnp.concatenate([blockdiag(wr), blockdiag(wi)], axis=-1).astype(BF16)


def _halve_gate_columns(w, split):
    scale = jnp.where(jnp.arange(w.shape[-1]) < split, 1.0, 0.5).astype(w.dtype)
    return (w * scale).astype(BF16)


def kernel(x, a_norm_g, a_w_in, a_conv_w, a_conv_b, a_gate_r_w, a_gate_r_b, a_gate_i_w,
           a_gate_i_b, a_lambda, a_w_out, kv_norm_g, w_kv, k_norm_g, b_norm_g, b_w_in,
           q_norm_g, sinks, b_w_out):
    n_a = a_w_in.shape[0]
    n_b = b_w_in.shape[0]
    row = lambda v: v[:, None, :]
    a_params = (
        row(a_norm_g), _halve_gate_columns(a_w_in, LRU_WIDTH), 0.5 * a_conv_w,
        row(0.5 * a_conv_b), _gate_group_weights(a_gate_r_w, a_gate_i_w),
        row(0.5 * a_gate_r_b), row(0.5 * a_gate_i_b), row(a_lambda), a_w_out.astype(BF16))
    kv_params = (kv_norm_g[None], w_kv.astype(BF16), jnp.tile(k_norm_g, N_KV_HEADS)[None])
    b_params = (row(b_norm_g), _halve_gate_columns(b_w_in, ATT_WIDTH),
                row(jnp.tile(q_norm_g, (1, 2))), b_w_out.astype(BF16))
    k4 = v4 = None
    for l in range(n_a):
        last = l == n_a - 1
        res = _rglru_layer(x, l, a_params, kv_params if last else None, permute_in=(l == 0))
        if last:
            x, k4, v4 = res
        else:
            x = res[0]
    sinks_flat = sinks.reshape(-1)
    for l in range(n_b):
        x = _swa_layer(x, k4, v4, l, b_params, sinks_flat, unpermute_out=(l == n_b - 1))
    return x
```

```python
import functools

import jax
import jax.numpy as jnp
from jax import lax
from jax.experimental import pallas as pl
from jax.experimental.pallas import tpu as pltpu

D_MODEL = 1024
LRU_WIDTH = 1536
LRU_BLOCKS = 16
LRU_BLOCK_W = LRU_WIDTH // LRU_BLOCKS
CONV_WIDTH = 4
LRU_C = 8.0
N_Q_HEADS = 16
N_KV_HEADS = 2
GROUP = N_Q_HEADS // N_KV_HEADS
HEAD_DIM = 64
ATT_WIDTH = N_Q_HEADS * HEAD_DIM
WINDOW = 128
BLOCK = 128
EPS = 1e-6

SUBLANES = 8
LANES = 128
STREAM_LEN = BLOCK // SUBLANES
CONV_TAIL_ROWS = (CONV_WIDTH - 1) * SUBLANES
SLAB_PITCH = 24
SLAB_BLOCK_ROWS = SUBLANES * SLAB_PITCH
GATE_GROUP_BLOCKS = 4
GATE_GROUP_W = GATE_GROUP_BLOCKS * LRU_BLOCK_W
N_GATE_GROUPS = LRU_BLOCKS // GATE_GROUP_BLOCKS
PAIRS_PER_KV = GROUP // 2
KV_LAYOUT_W = N_KV_HEADS * LANES

RGLRU_CHUNK = 1024
RGLRU_SUB_ROWS = 256
SWA_CHUNK = 1024
SWA_SUB_ROWS = 512
MXU_PIECE_W = 512
VMEM_LIMIT_BYTES = 56 * 1024 * 1024
F32_MIN_NORMAL = 1.17549435e-38

F32 = jnp.float32
BF16 = jnp.bfloat16


def _rmsnorm_rows(x, g):
    ms = jnp.mean(x * x, axis=-1, keepdims=True)
    return x * lax.rsqrt(ms + EPS) * g


def _softplus(z):
    return jnp.maximum(z, 0.0) + jnp.log1p(jnp.exp(-jnp.abs(z)))


def _slab_rows(n_rows):
    return n_rows // BLOCK * SLAB_BLOCK_ROWS


def _copy_time_rows(src, dst, n_rows, src_pitch, dst_pitch):
    for n in range(n_rows // BLOCK):
        for s in range(SUBLANES):
            a = n * SUBLANES * src_pitch + s * src_pitch
            b = n * SUBLANES * dst_pitch + s * dst_pitch
            dst(b, src(a))


def _load_permuted(x_ref, slab_ref, dst_ref):
    _, t, d = x_ref.shape
    for c in range(d // LANES):
        lanes = slice(c * LANES, (c + 1) * LANES)

        def put(b, val, c=c):
            slab_ref[c, b:b + STREAM_LEN, :] = val
        _copy_time_rows(lambda a, lanes=lanes: x_ref[0, a:a + STREAM_LEN, lanes], put,
                        t, STREAM_LEN, SLAB_PITCH)
        for n in range(t // BLOCK):
            for j in range(STREAM_LEN):
                r0 = n * BLOCK + j * SUBLANES
                dst_ref[r0:r0 + SUBLANES, lanes] = (
                    slab_ref[c, pl.ds(n * SLAB_BLOCK_ROWS + j, SUBLANES, stride=SLAB_PITCH), :])


def _scatter_unpermuted(slab_ref, val, row0, lane_tile0):
    rows, w = val.shape
    for c in range(w // LANES):
        for n in range(rows // BLOCK):
            for j in range(STREAM_LEN):
                r0 = n * BLOCK + j * SUBLANES
                slab_ref[lane_tile0 + c,
                         pl.ds(_slab_rows(row0) + n * SLAB_BLOCK_ROWS + j, SUBLANES,
                               stride=SLAB_PITCH), :] = (
                    val[r0:r0 + SUBLANES, c * LANES:(c + 1) * LANES])


def _store_from_slabs(o_ref, slab_ref):
    _, t, d = o_ref.shape
    for c in range(d // LANES):
        lanes = slice(c * LANES, (c + 1) * LANES)

        def put(b, val, lanes=lanes):
            o_ref[0, b:b + STREAM_LEN, lanes] = val
        _copy_time_rows(lambda a, c=c: slab_ref[c, a:a + STREAM_LEN, :], put,
                        t, SLAB_PITCH, STREAM_LEN)


def _dup_layout(t):
    low = lax.broadcasted_iota(jnp.int32, t.shape, 1) < HEAD_DIM
    swapped = pltpu.roll(t, HEAD_DIM, axis=1)
    return jnp.concatenate([jnp.where(low, t, swapped), jnp.where(low, swapped, t)], axis=1)


def _shared_kv(x, kvg_ref, wkv_ref, kg_ref):
    h = _rmsnorm_rows(x, kvg_ref[...]).astype(BF16)
    kv = jnp.dot(h, wkv_ref[...], preferred_element_type=F32)
    kw = N_KV_HEADS * HEAD_DIM
    k = kv[:, :kw]
    v = kv[:, kw:]
    low = lax.broadcasted_iota(jnp.int32, k.shape, 1) < HEAD_DIM
    k2 = k * k
    s_low = jnp.sum(jnp.where(low, k2, 0.0), axis=-1, keepdims=True)
    s_all = jnp.sum(k2, axis=-1, keepdims=True)
    ms = jnp.where(low, s_low, s_all - s_low) * (1.0 / HEAD_DIM)
    kn = k * lax.rsqrt(ms + EPS) * kg_ref[...] * (HEAD_DIM ** -0.5)
    return _dup_layout(kn).astype(BF16), _dup_layout(v).astype(BF16)


def _recurrence_unit(u_ref, y_ref, tail_ref, hcar_ref, convw_ref, convb_ref, wg_ref, brh_ref,
                     bih_ref, half_scale, row0, grp):
    lo = grp * GATE_GROUP_W
    xh = _conv_block(u_ref, tail_ref, convw_ref, convb_ref, row0, lo)
    zh = jnp.dot(jnp.concatenate(xh, axis=0).astype(BF16), wg_ref[grp],
                 preferred_element_type=F32)
    _scan_block(zh, xh, u_ref, y_ref, hcar_ref, brh_ref, bih_ref, half_scale, row0, lo)


def _conv_block(u_ref, tail_ref, convw_ref, convb_ref, row0, lo):
    w = GATE_GROUP_W
    hi = lo + w
    rows = lambda j: slice(row0 + j * SUBLANES, row0 + (j + 1) * SUBLANES)
    x_row = lambda j: u_ref[rows(j), lo:hi]

    n_tail = CONV_WIDTH - 1
    prev_tail = tail_ref[:, lo:hi].reshape(n_tail, SUBLANES, w)
    cur_tail = jnp.concatenate(
        [x_row(STREAM_LEN - n_tail + i) for i in range(n_tail)], axis=0).reshape(
            n_tail, SUBLANES, w)
    sub3 = lax.broadcasted_iota(jnp.int32, (n_tail, SUBLANES, w), 1)
    wrapped = pltpu.roll(jnp.where(sub3 == SUBLANES - 1, prev_tail, cur_tail), 1, axis=1)
    tail_ref[:, lo:hi] = cur_tail.reshape(CONV_TAIL_ROWS, w)
    window = [wrapped[i] for i in range(n_tail)]
    taps = [convw_ref[k:k + 1, lo:hi] for k in range(CONV_WIDTH)]
    conv_b = convb_ref[:, lo:hi]
    xh = []
    for j in range(STREAM_LEN):
        xj = x_row(j)
        acc = conv_b + taps[n_tail] * xj
        for k in range(n_tail):
            acc = acc + taps[k] * window[k]
        window = window[1:] + [xj]
        xh.append(acc)
    return xh


def _scan_block(zh, xh, u_ref, y_ref, hcar_ref, brh_ref, bih_ref, half_scale, row0, lo):
    w = GATE_GROUP_W
    hi = lo + w
    rows = lambda j: slice(row0 + j * SUBLANES, row0 + (j + 1) * SUBLANES)
    brh = brh_ref[:, lo:hi]
    bih = bih_ref[:, lo:hi]
    hs_c = half_scale[:, lo:hi]
    a_rows = []
    b_rows = []
    a_acc = h_acc = None
    for j in range(STREAM_LEN):
        r = slice(j * SUBLANES, (j + 1) * SUBLANES)
        t_r = jnp.tanh(zh[r, :w] + brh)
        t_i = jnp.tanh(zh[r, w:] + bih)
        log_a = hs_c * t_r + hs_c
        a = jnp.exp(log_a)
        one_m_a2 = (-1.0 - a * a) * jnp.tanh(log_a)
        root = one_m_a2 * lax.rsqrt(jnp.maximum(one_m_a2, F32_MIN_NORMAL))
        b = (root * xh[j]) * (1.0 + t_i)
        if j == 0:
            a_acc, h_acc = a, b
        else:
            h_acc = a * h_acc + b
            a_acc = a * a_acc
        a_rows.append(a)
        b_rows.append(b)

    sub = lax.broadcasted_iota(jnp.int32, (SUBLANES, w), 0)
    shift = 1
    while shift < SUBLANES:
        keep = sub >= shift
        a_prev = jnp.where(keep, pltpu.roll(a_acc, shift, axis=0), 1.0)
        h_prev = jnp.where(keep, pltpu.roll(h_acc, shift, axis=0), 0.0)
        h_acc = a_acc * h_prev + h_acc
        a_acc = a_acc * a_prev
        shift *= 2
    h_in = hcar_ref[:, lo:hi]
    ends = a_acc * h_in + h_acc
    hcar_ref[:, lo:hi] = ends[SUBLANES - 1:SUBLANES]
    h = jnp.where(sub == 0, h_in, pltpu.roll(ends, 1, axis=0))

    ys = []
    for j in range(STREAM_LEN):
        h = a_rows[j] * h + b_rows[j]
        gh = u_ref[rows(j), LRU_WIDTH + lo:LRU_WIDTH + hi]
        ys.append(h * (gh * (1.0 + jnp.tanh(gh))))
    y_ref[row0:row0 + BLOCK, lo:hi] = jnp.concatenate(ys, axis=0).astype(BF16)


def _interleave(units, pieces):
    slots = [[] for _ in units]
    for k, piece in enumerate(pieces):
        slots[(k * len(units)) // len(pieces)].append(piece)
    for unit, before in zip(units, slots):
        for piece in before:
            piece()
        unit()


def _run_pipeline(n_sub, in_pieces, mixer_units, out_pieces):
    for piece in in_pieces(0):
        piece()
    for sb in range(n_sub):
        pieces = in_pieces(sb + 1) if sb + 1 < n_sub else []
        if sb > 0:
            pieces = pieces + out_pieces(sb - 1)
        _interleave(mixer_units(sb), pieces)
    for piece in out_pieces(n_sub - 1):
        piece()


def _rglru_kernel(*refs, permute_in, emit_kv):
    (x_ref, g_ref, win_ref, convw_ref, convb_ref, wg_ref, brh_ref, bih_ref, lam_ref,
     wout_ref) = refs[:10]
    refs = refs[10:]
    if emit_kv:
        kvg_ref, wkv_ref, kg_ref = refs[:3]
        refs = refs[3:]
    o_ref = refs[0]
    refs = refs[1:]
    if emit_kv:
        k_ref, v_ref = refs[:2]
        refs = refs[2:]
    tail_ref, hcar_ref, hb_ref, u_ref, y_ref = refs[:5]
    chunk = x_ref.shape[1]
    sub_rows = RGLRU_SUB_ROWS

    @pl.when(pl.program_id(1) == 0)
    def _():
        tail_ref[...] = jnp.zeros_like(tail_ref)
        hcar_ref[...] = jnp.zeros_like(hcar_ref)

    if permute_in:
        slab_ref, xp_ref = refs[5:7]
        _load_permuted(x_ref, slab_ref, xp_ref)
        x_rows = lambda r0, cols=slice(None): xp_ref[r0:r0 + sub_rows, cols]
    else:
        x_rows = lambda r0, cols=slice(None): x_ref[0, r0:r0 + sub_rows, cols]

    half_scale = (-0.5 * LRU_C) * _softplus(-lam_ref[...])

    def in_proj_pieces(sb):
        r0 = sb * sub_rows
        slot = sb % 2

        def piece(c):
            if c == 0:
                hb_ref[...] = _rmsnorm_rows(x_rows(r0), g_ref[...]).astype(BF16)
            cols = slice(c * MXU_PIECE_W, (c + 1) * MXU_PIECE_W)
            u_ref[slot, :, cols] = jnp.dot(hb_ref[...], win_ref[:, cols],
                                           preferred_element_type=F32)
        return [functools.partial(piece, c) for c in range(2 * LRU_WIDTH // MXU_PIECE_W)]

    def recurrence_units(sb):
        slot = sb % 2
        return [functools.partial(
            _recurrence_unit, u_ref.at[slot], y_ref.at[slot], tail_ref, hcar_ref, convw_ref,
            convb_ref, wg_ref, brh_ref, bih_ref, half_scale, n * BLOCK, grp)
            for n in range(sub_rows // BLOCK) for grp in range(N_GATE_GROUPS)]

    def out_proj_pieces(sb):
        r0 = sb * sub_rows
        slot = sb % 2

        def piece(c):
            cols = slice(c * MXU_PIECE_W, (c + 1) * MXU_PIECE_W)
            o_ref[0, r0:r0 + sub_rows, cols] = (
                x_rows(r0, cols)
                + jnp.dot(y_ref[slot], wout_ref[:, cols], preferred_element_type=F32))

        def kv_piece():
            k4, v4 = _shared_kv(o_ref[0, r0:r0 + sub_rows, :], kvg_ref, wkv_ref, kg_ref)
            k_ref[0, r0:r0 + sub_rows, :] = k4
            v_ref[0, r0:r0 + sub_rows, :] = v4

        pieces = [functools.partial(piece, c) for c in range(D_MODEL // MXU_PIECE_W)]
        return pieces + ([kv_piece] if emit_kv else [])

    _run_pipeline(chunk // sub_rows, in_proj_pieces, recurrence_units, out_proj_pieces)


def _layer_spec(shape, layer):
    zeros = (0,) * len(shape)
    return pl.BlockSpec((None,) + tuple(shape), lambda *_: (layer,) + zeros,
                        pipeline_mode=pl.Buffered(1))


def _const_spec(shape):
    zeros = (0,) * len(shape)
    return pl.BlockSpec(tuple(shape), lambda *_: zeros, pipeline_mode=pl.Buffered(1))


def _rglru_layer(x, layer, params, kv_params, *, permute_in):
    bsz, seq, d = x.shape
    chunk = RGLRU_CHUNK
    sub_rows = RGLRU_SUB_ROWS
    emit_kv = kv_params is not None
    tile = lambda w: pl.BlockSpec((1, chunk, w), lambda b, t: (b, t, 0))
    in_specs = [tile(d)] + [_layer_spec(p.shape[1:], layer) for p in params]
    args = [x] + list(params)
    out_specs = [tile(d)]
    out_shape = [jax.ShapeDtypeStruct(x.shape, x.dtype)]
    if emit_kv:
        in_specs += [_const_spec(p.shape) for p in kv_params]
        args += list(kv_params)
        out_specs += [tile(KV_LAYOUT_W)] * 2
        out_shape += [jax.ShapeDtypeStruct((bsz, seq, KV_LAYOUT_W), BF16)] * 2
    scratch = [
        pltpu.VMEM((CONV_TAIL_ROWS, LRU_WIDTH), F32),
        pltpu.VMEM((1, LRU_WIDTH), F32),
        pltpu.VMEM((sub_rows, d), BF16),
        pltpu.VMEM((2, sub_rows, 2 * LRU_WIDTH), F32),
        pltpu.VMEM((2, sub_rows, LRU_WIDTH), BF16),
    ]
    if permute_in:
        scratch += [pltpu.VMEM((d // LANES, _slab_rows(chunk), LANES), F32),
                    pltpu.VMEM((chunk, d), F32)]
    return pl.pallas_call(
        functools.partial(_rglru_kernel, permute_in=permute_in, emit_kv=emit_kv),
        grid=(bsz, seq // chunk),
        in_specs=in_specs,
        out_specs=out_specs,
        out_shape=out_shape,
        scratch_shapes=scratch,
        compiler_params=pltpu.CompilerParams(
            dimension_semantics=("arbitrary", "arbitrary"),
            vmem_limit_bytes=VMEM_LIMIT_BYTES),
        name="rglru_kv_layer" if emit_kv else "rglru_layer",
    )(*args)


def _attention_unit(qn_ref, sg_ref, y_ref, kprev_ref, kcur_ref, vprev_ref, vcur_ref, sinks_ref,
                    sink_base, masks, first_chunk, row0, blk, j):
    from_prev, dist, low_o, ones = masks
    rows = slice(row0, row0 + BLOCK)
    pair_lanes = lambda i: slice((j * PAIRS_PER_KV + i) * LANES, (j * PAIRS_PER_KV + i + 1) * LANES)
    cols = slice(j * LANES, (j + 1) * LANES)

    def band(prev_ref, cur_ref):
        if blk == 0:
            return jnp.concatenate([prev_ref[0, :, cols], cur_ref[0, 0:BLOCK, cols]], axis=0)
        return cur_ref[0, (blk - 1) * BLOCK:(blk + 1) * BLOCK, cols]

    q_stack = jnp.concatenate(
        [qn_ref[rows, slice(e * ATT_WIDTH + pair_lanes(i).start, e * ATT_WIDTH + pair_lanes(i).stop)]
         for e in range(2) for i in range(PAIRS_PER_KV)], axis=0)
    s = lax.dot_general(q_stack, band(kprev_ref, kcur_ref), (((1,), (1,)), ((), ())),
                        preferred_element_type=F32)
    probs = []
    row_max = []
    for e in range(2):
        for i in range(PAIRS_PER_KV):
            head = j * GROUP + 2 * i + e
            slope = 2.0 ** (-8.0 * (head + 1) / N_Q_HEADS)
            r = e * PAIRS_PER_KV + i
            s_h = s[r * BLOCK:(r + 1) * BLOCK]
            sc = jnp.where(from_prev, s_h[:, :BLOCK], s_h[:, BLOCK:]) - slope * dist
            if blk == 0:
                sc = jnp.where(jnp.logical_and(from_prev, first_chunk), -jnp.inf, sc)
            m = jnp.maximum(jnp.max(sc, axis=-1, keepdims=True), sinks_ref[sink_base + head])
            pr = jnp.exp(sc - m)
            zero = jnp.zeros_like(pr)
            probs.append(jnp.concatenate(
                [jnp.where(from_prev, pr, zero), jnp.where(from_prev, zero, pr)],
                axis=1).astype(BF16))
            row_max.append(m)
    v_ext = jnp.concatenate([band(vprev_ref, vcur_ref), ones], axis=1)
    acc = jnp.dot(jnp.concatenate(probs, axis=0), v_ext, preferred_element_type=F32)
    for i in range(PAIRS_PER_KV):
        h_even = j * GROUP + 2 * i
        even = acc[i * BLOCK:(i + 1) * BLOCK]
        odd = acc[(PAIRS_PER_KV + i) * BLOCK:(PAIRS_PER_KV + i + 1) * BLOCK]
        m_pair = jnp.where(low_o, row_max[i], row_max[PAIRS_PER_KV + i])
        sink_pair = jnp.where(low_o, sinks_ref[sink_base + h_even],
                              sinks_ref[sink_base + h_even + 1])
        den = jnp.where(low_o, even[:, LANES:], odd[:, LANES:]) + jnp.exp(sink_pair - m_pair)
        o = jnp.where(low_o, even[:, :LANES], odd[:, :LANES]) * (1.0 / den)
        y_ref[rows, pair_lanes(i)] = (o * sg_ref[rows, pair_lanes(i)]).astype(BF16)


def _swa_kernel(sinks_ref, x_ref, g_ref, win_ref, qg_ref, kprev_ref, kcur_ref, vprev_ref,
                vcur_ref, wout_ref, o_ref, hb_ref, qn_ref, sg_ref, y_ref, *extra,
                sink_base, unpermute_out):
    chunk = x_ref.shape[1]
    sub_rows = SWA_SUB_ROWS
    first_chunk = pl.program_id(1) == 0

    ri = lax.broadcasted_iota(jnp.int32, (BLOCK, BLOCK), 0)
    ci = lax.broadcasted_iota(jnp.int32, (BLOCK, BLOCK), 1)
    q_time = (ri & (SUBLANES - 1)) * STREAM_LEN + (ri >> 3)
    k_time = (ci & (SUBLANES - 1)) * STREAM_LEN + (ci >> 3)
    from_prev = k_time > q_time
    dist = ((q_time - k_time) & (BLOCK - 1)).astype(F32)
    low_o = ci < HEAD_DIM
    masks = (from_prev, dist, low_o, jnp.ones((2 * BLOCK, LANES), BF16))
    low_q = lax.broadcasted_iota(jnp.int32, (sub_rows, LANES), 1) < HEAD_DIM

    def in_proj_pieces(sb):
        r0 = sb * sub_rows
        slot = sb % 2

        def piece(c):
            if c == 0:
                hb_ref[...] = _rmsnorm_rows(x_ref[0, r0:r0 + sub_rows, :], g_ref[...]).astype(BF16)
            cols = slice(c * MXU_PIECE_W, (c + 1) * MXU_PIECE_W)
            u = jnp.dot(hb_ref[...], win_ref[:, cols], preferred_element_type=F32)
            if c < ATT_WIDTH // MXU_PIECE_W:
                for p in range(MXU_PIECE_W // LANES):
                    q = u[:, p * LANES:(p + 1) * LANES]
                    q2 = q * q
                    s_low = jnp.sum(jnp.where(low_q, q2, 0.0), axis=-1, keepdims=True)
                    s_all = jnp.sum(q2, axis=-1, keepdims=True)
                    ms = jnp.where(low_q, s_low, s_all - s_low) * (1.0 / HEAD_DIM)
                    qn = q * lax.rsqrt(ms + EPS) * qg_ref[...]
                    l0 = c * MXU_PIECE_W + p * LANES
                    qn_ref[slot, :, l0:l0 + LANES] = jnp.where(low_q, qn, 0.0).astype(BF16)
                    qn_ref[slot, :, ATT_WIDTH + l0:ATT_WIDTH + l0 + LANES] = (
                        jnp.where(low_q, 0.0, qn).astype(BF16))
            else:
                lo = c * MXU_PIECE_W - ATT_WIDTH
                sg_ref[slot, :, lo:lo + MXU_PIECE_W] = u * (1.0 + jnp.tanh(u))
        return [functools.partial(piece, c) for c in range(2 * ATT_WIDTH // MXU_PIECE_W)]

    def attention_units(sb):
        slot = sb % 2
        return [functools.partial(
            _attention_unit, qn_ref.at[slot], sg_ref.at[slot], y_ref.at[slot], kprev_ref,
            kcur_ref, vprev_ref, vcur_ref, sinks_ref, sink_base, masks, first_chunk,
            n * BLOCK, sb * (sub_rows // BLOCK) + n, j)
            for n in range(sub_rows // BLOCK) for j in range(N_KV_HEADS)]

    def out_proj_pieces(sb):
        r0 = sb * sub_rows
        slot = sb % 2

        def piece(c):
            cols = slice(c * MXU_PIECE_W, (c + 1) * MXU_PIECE_W)
            out = (x_ref[0, r0:r0 + sub_rows, cols]
                   + jnp.dot(y_ref[slot], wout_ref[:, cols], preferred_element_type=F32))
            if unpermute_out:
                _scatter_unpermuted(extra[0], out, r0, c * MXU_PIECE_W // LANES)
            else:
                o_ref[0, r0:r0 + sub_rows, cols] = out
        return [functools.partial(piece, c) for c in range(D_MODEL // MXU_PIECE_W)]

    _run_pipeline(chunk // sub_rows, in_proj_pieces, attention_units, out_proj_pieces)
    if unpermute_out:
        _store_from_slabs(o_ref, extra[0])


def _swa_layer(x, k4, v4, layer, params, sinks_flat, *, unpermute_out):
    bsz, seq, d = x.shape
    chunk = SWA_CHUNK
    sub_rows = SWA_SUB_ROWS
    blocks_per_chunk = chunk // BLOCK
    norm_g, w_in, q_norm_g2, w_out = params
    prev_spec = pl.BlockSpec(
        (1, BLOCK, KV_LAYOUT_W),
        lambda b, t, s: (b, jnp.maximum(t * blocks_per_chunk - 1, 0), 0))
    cur_spec = pl.BlockSpec((1, chunk, KV_LAYOUT_W), lambda b, t, s: (b, t, 0))
    scratch = [
        pltpu.VMEM((sub_rows, d), BF16),
        pltpu.VMEM((2, sub_rows, 2 * ATT_WIDTH), BF16),
        pltpu.VMEM((2, sub_rows, ATT_WIDTH), F32),
        pltpu.VMEM((2, sub_rows, ATT_WIDTH), BF16),
    ]
    if unpermute_out:
        scratch.append(pltpu.VMEM((d // LANES, _slab_rows(chunk), LANES), F32))
    grid_spec = pltpu.PrefetchScalarGridSpec(
        num_scalar_prefetch=1,
        grid=(bsz, seq // chunk),
        in_specs=[
            pl.BlockSpec((1, chunk, d), lambda b, t, s: (b, t, 0)),
            _layer_spec(norm_g.shape[1:], layer),
            _layer_spec(w_in.shape[1:], layer),
            _layer_spec(q_norm_g2.shape[1:], layer),
            prev_spec, cur_spec, prev_spec, cur_spec,
            _layer_spec(w_out.shape[1:], layer),
        ],
        out_specs=pl.BlockSpec((1, chunk, d), lambda b, t, s: (b, t, 0)),
        scratch_shapes=scratch,
    )
    return pl.pallas_call(
        functools.partial(_swa_kernel, sink_base=layer * N_Q_HEADS, unpermute_out=unpermute_out),
        grid_spec=grid_spec,
        out_shape=jax.ShapeDtypeStruct(x.shape, x.dtype),
        compiler_params=pltpu.CompilerParams(
            dimension_semantics=("arbitrary", "arbitrary"),
            vmem_limit_bytes=VMEM_LIMIT_BYTES),
        name="swa_out_layer" if unpermute_out else "swa_layer",
    )(sinks_flat, x, norm_g, w_in, q_norm_g2, k4, k4, v4, v4, w_out)


def _gate_group_weights(wr, wi):
    def blockdiag(w):
        n_l = w.shape[0]
        w = w.reshape(n_l, N_GATE_GROUPS, GATE_GROUP_BLOCKS, LRU_BLOCK_W, LRU_BLOCK_W)
        eye = jnp.eye(GATE_GROUP_BLOCKS, dtype=w.dtype)
        full = w[:, :, :, :, None, :] * eye[None, None, :, None, :, None]
        return full.reshape(n_l, N_GATE_GROUPS, GATE_GROUP_W, GATE_GROUP_W)
    return jnp.concatenate([blockdiag(wr), blockdiag(wi)], axis=-1).astype(BF16)


def _halve_gate_columns(w, split):
    scale = jnp.where(jnp.arange(w.shape[-1]) < split, 1.0, 0.5).astype(w.dtype)
    return (w * scale).astype(BF16)


def kernel(x, a_norm_g, a_w_in, a_conv_w, a_conv_b, a_gate_r_w, a_gate_r_b, a_gate_i_w,
           a_gate_i_b, a_lambda, a_w_out, kv_norm_g, w_kv, k_norm_g, b_norm_g, b_w_in,
           q_norm_g, sinks, b_w_out):
    n_a = a_w_in.shape[0]
    n_b = b_w_in.shape[0]
    row = lambda v: v[:, None, :]
    a_params = (
        row(a_norm_g), _halve_gate_columns(a_w_in, LRU_WIDTH), 0.5 * a_conv_w,
        row(0.5 * a_conv_b), _gate_group_weights(a_gate_r_w, a_gate_i_w),
        row(0.5 * a_gate_r_b), row(0.5 * a_gate_i_b), row(a_lambda), a_w_out.astype(BF16))
    kv_params = (kv_norm_g[None], w_kv.astype(BF16), jnp.tile(k_norm_g, N_KV_HEADS)[None])
    b_params = (row(b_norm_g), _halve_gate_columns(b_w_in, ATT_WIDTH),
                row(jnp.tile(q_norm_g, (1, 2))), b_w_out.astype(BF16))
    k4 = v4 = None
    for l in range(n_a):
        last = l == n_a - 1
        res = _rglru_layer(x, l, a_params, kv_params if last else None, permute_in=(l == 0))
        if last:
            x, k4, v4 = res
        else:
            x = res[0]
    sinks_flat = sinks.reshape(-1)
    for l in range(n_b):
        x = _swa_layer(x, k4, v4, l, b_params, sinks_flat, unpermute_out=(l == n_b - 1))
    return x
```

```python
import functools

import jax
import jax.numpy as jnp
from jax import lax
from jax.experimental import pallas as pl
from jax.experimental.pallas import tpu as pltpu

D_MODEL = 1024
LRU_WIDTH = 1536
LRU_BLOCKS = 16
LRU_BLOCK_W = LRU_WIDTH // LRU_BLOCKS
CONV_WIDTH = 4
LRU_C = 8.0
N_Q_HEADS = 16
N_KV_HEADS = 2
GROUP = N_Q_HEADS // N_KV_HEADS
HEAD_DIM = 64
ATT_WIDTH = N_Q_HEADS * HEAD_DIM
WINDOW = 128
BLOCK = 128
EPS = 1e-6

SUBLANES = 8
LANES = 128
STREAM_LEN = BLOCK // SUBLANES
CONV_TAIL_ROWS = (CONV_WIDTH - 1) * SUBLANES
SLAB_PITCH = 24
SLAB_BLOCK_ROWS = SUBLANES * SLAB_PITCH
GATE_GROUP_BLOCKS = 4
GATE_GROUP_W = GATE_GROUP_BLOCKS * LRU_BLOCK_W
N_GATE_GROUPS = LRU_BLOCKS // GATE_GROUP_BLOCKS
PAIRS_PER_KV = GROUP // 2
KV_LAYOUT_W = N_KV_HEADS * LANES

RGLRU_CHUNK = 1024
RGLRU_SUB_ROWS = 512
RGLRU_KV_SUB_ROWS = 256
SWA_CHUNK = 1024
SWA_SUB_ROWS = 512
RGLRU_PIECE_W = 256
RGLRU_KV_PIECE_W = 512
SWA_PIECE_W = 256
VMEM_LIMIT_BYTES = 56 * 1024 * 1024
F32_MIN_NORMAL = 1.17549435e-38

F32 = jnp.float32
BF16 = jnp.bfloat16


def _rmsnorm_rows(x, g):
    ms = jnp.mean(x * x, axis=-1, keepdims=True)
    return x * lax.rsqrt(ms + EPS) * g


def _softplus(z):
    return jnp.maximum(z, 0.0) + jnp.log1p(jnp.exp(-jnp.abs(z)))


def _slab_rows(n_rows):
    return n_rows // BLOCK * SLAB_BLOCK_ROWS


def _copy_time_rows(src, dst, n_rows, src_pitch, dst_pitch):
    for n in range(n_rows // BLOCK):
        for s in range(SUBLANES):
            a = n * SUBLANES * src_pitch + s * src_pitch
            b = n * SUBLANES * dst_pitch + s * dst_pitch
            dst(b, src(a))


def _load_permuted(x_ref, src_row0, slab_ref, dst_ref, dst_row0, n_rows):
    d = x_ref.shape[2]
    for c in range(d // LANES):
        lanes = slice(c * LANES, (c + 1) * LANES)

        def put(b, val, c=c):
            slab_ref[c, b:b + STREAM_LEN, :] = val
        _copy_time_rows(
            lambda a, lanes=lanes: x_ref[0, src_row0 + a:src_row0 + a + STREAM_LEN, lanes], put,
            n_rows, STREAM_LEN, SLAB_PITCH)
        for n in range(n_rows // BLOCK):
            for j in range(STREAM_LEN):
                r0 = dst_row0 + n * BLOCK + j * SUBLANES
                dst_ref[r0:r0 + SUBLANES, lanes] = (
                    slab_ref[c, pl.ds(n * SLAB_BLOCK_ROWS + j, SUBLANES, stride=SLAB_PITCH), :])


def _scatter_unpermuted(slab_ref, val, row0, lane_tile0):
    rows, w = val.shape
    for c in range(w // LANES):
        for n in range(rows // BLOCK):
            for j in range(STREAM_LEN):
                r0 = n * BLOCK + j * SUBLANES
                slab_ref[lane_tile0 + c,
                         pl.ds(_slab_rows(row0) + n * SLAB_BLOCK_ROWS + j, SUBLANES,
                               stride=SLAB_PITCH), :] = (
                    val[r0:r0 + SUBLANES, c * LANES:(c + 1) * LANES])


def _store_from_slabs(o_ref, slab_ref):
    _, t, d = o_ref.shape
    for c in range(d // LANES):
        lanes = slice(c * LANES, (c + 1) * LANES)

        def put(b, val, lanes=lanes):
            o_ref[0, b:b + STREAM_LEN, lanes] = val
        _copy_time_rows(lambda a, c=c: slab_ref[c, a:a + STREAM_LEN, :], put,
                        t, SLAB_PITCH, STREAM_LEN)


def _dup_layout(t):
    low = lax.broadcasted_iota(jnp.int32, t.shape, 1) < HEAD_DIM
    swapped = pltpu.roll(t, HEAD_DIM, axis=1)
    return jnp.concatenate([jnp.where(low, t, swapped), jnp.where(low, swapped, t)], axis=1)


def _shared_kv(x, kvg_ref, wkv_ref, kg_ref):
    h = _rmsnorm_rows(x, kvg_ref[...]).astype(BF16)
    kv = jnp.dot(h, wkv_ref[...], preferred_element_type=F32)
    kw = N_KV_HEADS * HEAD_DIM
    k = kv[:, :kw]
    v = kv[:, kw:]
    low = lax.broadcasted_iota(jnp.int32, k.shape, 1) < HEAD_DIM
    k2 = k * k
    s_low = jnp.sum(jnp.where(low, k2, 0.0), axis=-1, keepdims=True)
    s_all = jnp.sum(k2, axis=-1, keepdims=True)
    ms = jnp.where(low, s_low, s_all - s_low) * (1.0 / HEAD_DIM)
    kn = k * lax.rsqrt(ms + EPS) * kg_ref[...] * (HEAD_DIM ** -0.5)
    return _dup_layout(kn).astype(BF16), _dup_layout(v).astype(BF16)


def _recurrence_unit(u_ref, y_ref, tail_ref, hcar_ref, convw_ref, convb_ref, wg_ref, brh_ref,
                     bih_ref, half_scale, row0, grp):
    lo = grp * GATE_GROUP_W
    xh = _conv_block(u_ref, tail_ref, convw_ref, convb_ref, row0, lo)
    zh = jnp.dot(jnp.concatenate(xh, axis=0).astype(BF16), wg_ref[grp],
                 preferred_element_type=F32)
    _scan_block(zh, xh, u_ref, y_ref, hcar_ref, brh_ref, bih_ref, half_scale, row0, lo)


def _conv_block(u_ref, tail_ref, convw_ref, convb_ref, row0, lo):
    w = GATE_GROUP_W
    hi = lo + w
    rows = lambda j: slice(row0 + j * SUBLANES, row0 + (j + 1) * SUBLANES)
    x_row = lambda j: u_ref[rows(j), lo:hi]

    n_tail = CONV_WIDTH - 1
    prev_tail = tail_ref[:, lo:hi].reshape(n_tail, SUBLANES, w)
    cur_tail = jnp.concatenate(
        [x_row(STREAM_LEN - n_tail + i) for i in range(n_tail)], axis=0).reshape(
            n_tail, SUBLANES, w)
    sub3 = lax.broadcasted_iota(jnp.int32, (n_tail, SUBLANES, w), 1)
    wrapped = pltpu.roll(jnp.where(sub3 == SUBLANES - 1, prev_tail, cur_tail), 1, axis=1)
    tail_ref[:, lo:hi] = cur_tail.reshape(CONV_TAIL_ROWS, w)
    window = [wrapped[i] for i in range(n_tail)]
    taps = [convw_ref[k:k + 1, lo:hi] for k in range(CONV_WIDTH)]
    conv_b = convb_ref[:, lo:hi]
    xh = []
    for j in range(STREAM_LEN):
        xj = x_row(j)
        acc = conv_b + taps[n_tail] * xj
        for k in range(n_tail):
            acc = acc + taps[k] * window[k]
        window = window[1:] + [xj]
        xh.append(acc)
    return xh


def _scan_block(zh, xh, u_ref, y_ref, hcar_ref, brh_ref, bih_ref, half_scale, row0, lo):
    w = GATE_GROUP_W
    hi = lo + w
    rows = lambda j: slice(row0 + j * SUBLANES, row0 + (j + 1) * SUBLANES)
    brh = brh_ref[:, lo:hi]
    bih = bih_ref[:, lo:hi]
    hs_c = half_scale[:, lo:hi]
    a_rows = []
    b_rows = []
    a_acc = h_acc = None
    for j in range(STREAM_LEN):
        r = slice(j * SUBLANES, (j + 1) * SUBLANES)
        t_r = jnp.tanh(zh[r, :w] + brh)
        t_i = jnp.tanh(zh[r, w:] + bih)
        log_a = hs_c * t_r + hs_c
        a = jnp.exp(log_a)
        one_m_a2 = (-1.0 - a * a) * jnp.tanh(log_a)
        root = one_m_a2 * lax.rsqrt(jnp.maximum(one_m_a2, F32_MIN_NORMAL))
        b = (root * xh[j]) * (1.0 + t_i)
        if j == 0:
            a_acc, h_acc = a, b
        else:
            h_acc = a * h_acc + b
            a_acc = a * a_acc
        a_rows.append(a)
        b_rows.append(b)

    sub = lax.broadcasted_iota(jnp.int32, (SUBLANES, w), 0)
    shift = 1
    while shift < SUBLANES:
        keep = sub >= shift
        a_prev = jnp.where(keep, pltpu.roll(a_acc, shift, axis=0), 1.0)
        h_prev = jnp.where(keep, pltpu.roll(h_acc, shift, axis=0), 0.0)
        h_acc = a_acc * h_prev + h_acc
        a_acc = a_acc * a_prev
        shift *= 2
    h_in = hcar_ref[:, lo:hi]
    ends = a_acc * h_in + h_acc
    hcar_ref[:, lo:hi] = ends[SUBLANES - 1:SUBLANES]
    h = jnp.where(sub == 0, h_in, pltpu.roll(ends, 1, axis=0))

    ys = []
    for j in range(STREAM_LEN):
        h = a_rows[j] * h + b_rows[j]
        gh = u_ref[rows(j), LRU_WIDTH + lo:LRU_WIDTH + hi]
        ys.append(h * (gh * (1.0 + jnp.tanh(gh))))
    y_ref[row0:row0 + BLOCK, lo:hi] = jnp.concatenate(ys, axis=0).astype(BF16)


def _interleave(units, pieces):
    slots = [[] for _ in units]
    for k, piece in enumerate(pieces):
        slots[(k * len(units)) // len(pieces)].append(piece)
    for unit, before in zip(units, slots):
        for piece in before:
            piece()
        unit()


def _run_pipeline(n_sub, in_pieces, mixer_units, out_pieces):
    for piece in in_pieces(0):
        piece()
    for sb in range(n_sub):
        pieces = in_pieces(sb + 1) if sb + 1 < n_sub else []
        if sb > 0:
            pieces = pieces + out_pieces(sb - 1)
        _interleave(mixer_units(sb), pieces)
    for piece in out_pieces(n_sub - 1):
        piece()


def _rglru_kernel(*refs, permute_in, emit_kv):
    (x_ref, g_ref, win_ref, convw_ref, convb_ref, wg_ref, brh_ref, bih_ref, lam_ref,
     wout_ref) = refs[:10]
    refs = refs[10:]
    if emit_kv:
        kvg_ref, wkv_ref, kg_ref = refs[:3]
        refs = refs[3:]
    o_ref = refs[0]
    refs = refs[1:]
    if emit_kv:
        k_ref, v_ref = refs[:2]
        refs = refs[2:]
    tail_ref, hcar_ref, hb_ref = refs[:3]
    u_slots, y_slots = refs[3:5], refs[5:7]
    chunk = x_ref.shape[1]
    sub_rows = hb_ref.shape[0]
    piece_w = RGLRU_KV_PIECE_W if emit_kv else RGLRU_PIECE_W
    n_sub = chunk // sub_rows

    if permute_in:
        slab_ref, xp_ref = refs[7:9]
        x_rows = lambda r0, cols=slice(None): xp_ref[r0:r0 + sub_rows, cols]
    else:
        x_rows = lambda r0, cols=slice(None): x_ref[0, r0:r0 + sub_rows, cols]

    half_scale = (-0.5 * LRU_C) * _softplus(-lam_ref[...])

    def in_proj_pieces(sb):
        r0 = sb * sub_rows
        slot = sb % 2

        def piece(c):
            if c == 0:
                if permute_in:
                    _load_permuted(x_ref, r0, slab_ref, xp_ref, r0, sub_rows)
                hb_ref[...] = _rmsnorm_rows(x_rows(r0), g_ref[...]).astype(BF16)
            cols = slice(c * piece_w, (c + 1) * piece_w)
            u_slots[slot][:, cols] = jnp.dot(hb_ref[...], win_ref[:, cols],
                                             preferred_element_type=F32)
        return [functools.partial(piece, c) for c in range(2 * LRU_WIDTH // piece_w)]

    def recurrence_units(sb):
        slot = sb % 2
        return [functools.partial(
            _recurrence_unit, u_slots[slot], y_slots[slot], tail_ref, hcar_ref, convw_ref,
            convb_ref, wg_ref, brh_ref, bih_ref, half_scale, n * BLOCK, grp)
            for n in range(sub_rows // BLOCK) for grp in range(N_GATE_GROUPS)]

    def out_proj_pieces(sb):
        r0 = sb * sub_rows
        slot = sb % 2

        def piece(c):
            cols = slice(c * piece_w, (c + 1) * piece_w)
            o_ref[0, r0:r0 + sub_rows, cols] = (
                x_rows(r0, cols)
                + jnp.dot(y_slots[slot][...], wout_ref[:, cols], preferred_element_type=F32))

        def kv_piece():
            k4, v4 = _shared_kv(o_ref[0, r0:r0 + sub_rows, :], kvg_ref, wkv_ref, kg_ref)
            k_ref[0, r0:r0 + sub_rows, :] = k4
            v_ref[0, r0:r0 + sub_rows, :] = v4

        pieces = [functools.partial(piece, c) for c in range(D_MODEL // piece_w)]
        return pieces + ([kv_piece] if emit_kv else [])

    @pl.when(pl.program_id(1) == 0)
    def _():
        tail_ref[...] = jnp.zeros_like(tail_ref)
        hcar_ref[...] = jnp.zeros_like(hcar_ref)

    _run_pipeline(n_sub, in_proj_pieces, recurrence_units, out_proj_pieces)


def _layer_spec(shape, layer):
    zeros = (0,) * len(shape)
    return pl.BlockSpec((None,) + tuple(shape), lambda *_: (layer,) + zeros,
                        pipeline_mode=pl.Buffered(1))


def _const_spec(shape):
    zeros = (0,) * len(shape)
    return pl.BlockSpec(tuple(shape), lambda *_: zeros, pipeline_mode=pl.Buffered(1))


def _rglru_layer(x, layer, params, kv_params, *, permute_in):
    bsz, seq, d = x.shape
    chunk = RGLRU_CHUNK
    emit_kv = kv_params is not None
    sub_rows = RGLRU_KV_SUB_ROWS if emit_kv else RGLRU_SUB_ROWS
    tile = lambda w: pl.BlockSpec((1, chunk, w), lambda b, t: (b, t, 0))
    in_specs = [tile(d)] + [_layer_spec(p.shape[1:], layer) for p in params]
    args = [x] + list(params)
    out_specs = [tile(d)]
    out_shape = [jax.ShapeDtypeStruct(x.shape, x.dtype)]
    if emit_kv:
        in_specs += [_const_spec(p.shape) for p in kv_params]
        args += list(kv_params)
        out_specs += [tile(KV_LAYOUT_W)] * 2
        out_shape += [jax.ShapeDtypeStruct((bsz, seq, KV_LAYOUT_W), BF16)] * 2
    scratch = [
        pltpu.VMEM((CONV_TAIL_ROWS, LRU_WIDTH), F32),
        pltpu.VMEM((1, LRU_WIDTH), F32),
        pltpu.VMEM((sub_rows, d), BF16),
    ]
    scratch += [pltpu.VMEM((sub_rows, 2 * LRU_WIDTH), F32)] * 2
    scratch += [pltpu.VMEM((sub_rows, LRU_WIDTH), BF16)] * 2
    if permute_in:
        scratch += [pltpu.VMEM((d // LANES, _slab_rows(sub_rows), LANES), F32),
                    pltpu.VMEM((chunk, d), F32)]
    return pl.pallas_call(
        functools.partial(_rglru_kernel, permute_in=permute_in, emit_kv=emit_kv),
        grid=(bsz, seq // chunk),
        in_specs=in_specs,
        out_specs=out_specs,
        out_shape=out_shape,
        scratch_shapes=scratch,
        compiler_params=pltpu.CompilerParams(
            dimension_semantics=("arbitrary", "arbitrary"),
            vmem_limit_bytes=VMEM_LIMIT_BYTES),
        name="rglru_kv_layer" if emit_kv else "rglru_layer",
    )(*args)


def _attention_unit(qn_ref, sg_ref, y_ref, kprev_ref, kcur_ref, vprev_ref, vcur_ref, sinks_ref,
                    sink_base, masks, first_chunk, row0, blk, j):
    from_prev, dist, low_o, ones = masks
    rows = slice(row0, row0 + BLOCK)
    pair_lanes = lambda i: slice((j * PAIRS_PER_KV + i) * LANES, (j * PAIRS_PER_KV + i + 1) * LANES)
    cols = slice(j * LANES, (j + 1) * LANES)

    def band(prev_ref, cur_ref):
        if blk == 0:
            return jnp.concatenate([prev_ref[0, :, cols], cur_ref[0, 0:BLOCK, cols]], axis=0)
        return cur_ref[0, (blk - 1) * BLOCK:(blk + 1) * BLOCK, cols]

    q_stack = jnp.concatenate(
        [qn_ref[rows, slice(e * ATT_WIDTH + pair_lanes(i).start, e * ATT_WIDTH + pair_lanes(i).stop)]
         for e in range(2) for i in range(PAIRS_PER_KV)], axis=0)
    s = lax.dot_general(q_stack, band(kprev_ref, kcur_ref), (((1,), (1,)), ((), ())),
                        preferred_element_type=F32)
    probs = []
    row_max = []
    for e in range(2):
        for i in range(PAIRS_PER_KV):
            head = j * GROUP + 2 * i + e
            slope = 2.0 ** (-8.0 * (head + 1) / N_Q_HEADS)
            r = e * PAIRS_PER_KV + i
            s_h = s[r * BLOCK:(r + 1) * BLOCK]
            sc = jnp.where(from_prev, s_h[:, :BLOCK], s_h[:, BLOCK:]) - slope * dist
            if blk == 0:
                sc = jnp.where(jnp.logical_and(from_prev, first_chunk), -jnp.inf, sc)
            m = jnp.maximum(jnp.max(sc, axis=-1, keepdims=True), sinks_ref[sink_base + head])
            pr = jnp.exp(sc - m)
            zero = jnp.zeros_like(pr)
            probs.append(jnp.concatenate(
                [jnp.where(from_prev, pr, zero), jnp.where(from_prev, zero, pr)],
                axis=1).astype(BF16))
            row_max.append(m)
    v_ext = jnp.concatenate([band(vprev_ref, vcur_ref), ones], axis=1)
    acc = jnp.dot(jnp.concatenate(probs, axis=0), v_ext, preferred_element_type=F32)
    for i in range(PAIRS_PER_KV):
        h_even = j * GROUP + 2 * i
        even = acc[i * BLOCK:(i + 1) * BLOCK]
        odd = acc[(PAIRS_PER_KV + i) * BLOCK:(PAIRS_PER_KV + i + 1) * BLOCK]
        m_pair = jnp.where(low_o, row_max[i], row_max[PAIRS_PER_KV + i])
        sink_pair = jnp.where(low_o, sinks_ref[sink_base + h_even],
                              sinks_ref[sink_base + h_even + 1])
        den = jnp.where(low_o, even[:, LANES:], odd[:, LANES:]) + jnp.exp(sink_pair - m_pair)
        o = jnp.where(low_o, even[:, :LANES], odd[:, :LANES]) * (1.0 / den)
        y_ref[rows, pair_lanes(i)] = (o * sg_ref[rows, pair_lanes(i)]).astype(BF16)


def _swa_kernel(sinks_ref, x_ref, g_ref, win_ref, qg_ref, kprev_ref, kcur_ref, vprev_ref,
                vcur_ref, wout_ref, o_ref, hb_ref, *scratch, sink_base, unpermute_out):
    chunk = x_ref.shape[1]
    sub_rows = SWA_SUB_ROWS
    piece_w = SWA_PIECE_W
    qn_slots, sg_slots, y_slots, extra = scratch[0:2], scratch[2:4], scratch[4:6], scratch[6:]
    first_chunk = pl.program_id(1) == 0

    ri = lax.broadcasted_iota(jnp.int32, (BLOCK, BLOCK), 0)
    ci = lax.broadcasted_iota(jnp.int32, (BLOCK, BLOCK), 1)
    q_time = (ri & (SUBLANES - 1)) * STREAM_LEN + (ri >> 3)
    k_time = (ci & (SUBLANES - 1)) * STREAM_LEN + (ci >> 3)
    from_prev = k_time > q_time
    dist = ((q_time - k_time) & (BLOCK - 1)).astype(F32)
    low_o = ci < HEAD_DIM
    masks = (from_prev, dist, low_o, jnp.ones((2 * BLOCK, LANES), BF16))
    low_q = lax.broadcasted_iota(jnp.int32, (sub_rows, LANES), 1) < HEAD_DIM

    def in_proj_pieces(sb):
        r0 = sb * sub_rows
        slot = sb % 2

        def piece(c):
            if c == 0:
                hb_ref[...] = _rmsnorm_rows(x_ref[0, r0:r0 + sub_rows, :], g_ref[...]).astype(BF16)
            cols = slice(c * piece_w, (c + 1) * piece_w)
            u = jnp.dot(hb_ref[...], win_ref[:, cols], preferred_element_type=F32)
            if c < ATT_WIDTH // piece_w:
                for p in range(piece_w // LANES):
                    q = u[:, p * LANES:(p + 1) * LANES]
                    q2 = q * q
                    s_low = jnp.sum(jnp.where(low_q, q2, 0.0), axis=-1, keepdims=True)
                    s_all = jnp.sum(q2, axis=-1, keepdims=True)
                    ms = jnp.where(low_q, s_low, s_all - s_low) * (1.0 / HEAD_DIM)
                    qn = q * lax.rsqrt(ms + EPS) * qg_ref[...]
                    l0 = c * piece_w + p * LANES
                    qn_slots[slot][:, l0:l0 + LANES] = jnp.where(low_q, qn, 0.0).astype(BF16)
                    qn_slots[slot][:, ATT_WIDTH + l0:ATT_WIDTH + l0 + LANES] = (
                        jnp.where(low_q, 0.0, qn).astype(BF16))
            else:
                lo = c * piece_w - ATT_WIDTH
                sg_slots[slot][:, lo:lo + piece_w] = u * (1.0 + jnp.tanh(u))
        return [functools.partial(piece, c) for c in range(2 * ATT_WIDTH // piece_w)]

    def attention_units(sb):
        slot = sb % 2
        return [functools.partial(
            _attention_unit, qn_slots[slot], sg_slots[slot], y_slots[slot], kprev_ref,
            kcur_ref, vprev_ref, vcur_ref, sinks_ref, sink_base, masks, first_chunk,
            n * BLOCK, sb * (sub_rows // BLOCK) + n, j)
            for n in range(sub_rows // BLOCK) for j in range(N_KV_HEADS)]

    def out_proj_pieces(sb):
        r0 = sb * sub_rows
        slot = sb % 2

        def piece(c):
            cols = slice(c * piece_w, (c + 1) * piece_w)
            out = (x_ref[0, r0:r0 + sub_rows, cols]
                   + jnp.dot(y_slots[slot][...], wout_ref[:, cols], preferred_element_type=F32))
            if unpermute_out:
                _scatter_unpermuted(extra[0], out, r0, c * piece_w // LANES)
            else:
                o_ref[0, r0:r0 + sub_rows, cols] = out
        return [functools.partial(piece, c) for c in range(D_MODEL // piece_w)]

    _run_pipeline(chunk // sub_rows, in_proj_pieces, attention_units, out_proj_pieces)
    if unpermute_out:
        _store_from_slabs(o_ref, extra[0])


def _swa_layer(x, k4, v4, layer, params, sinks_flat, *, unpermute_out):
    bsz, seq, d = x.shape
    chunk = SWA_CHUNK
    sub_rows = SWA_SUB_ROWS
    blocks_per_chunk = chunk // BLOCK
    norm_g, w_in, q_norm_g2, w_out = params
    prev_spec = pl.BlockSpec(
        (1, BLOCK, KV_LAYOUT_W),
        lambda b, t, s: (b, jnp.maximum(t * blocks_per_chunk - 1, 0), 0))
    cur_spec = pl.BlockSpec((1, chunk, KV_LAYOUT_W), lambda b, t, s: (b, t, 0))
    scratch = [
        pltpu.VMEM((sub_rows, d), BF16),
    ]
    scratch += [pltpu.VMEM((sub_rows, 2 * ATT_WIDTH), BF16)] * 2
    scratch += [pltpu.VMEM((sub_rows, ATT_WIDTH), F32)] * 2
    scratch += [pltpu.VMEM((sub_rows, ATT_WIDTH), BF16)] * 2
    if unpermute_out:
        scratch.append(pltpu.VMEM((d // LANES, _slab_rows(chunk), LANES), F32))
    grid_spec = pltpu.PrefetchScalarGridSpec(
        num_scalar_prefetch=1,
        grid=(bsz, seq // chunk),
        in_specs=[
            pl.BlockSpec((1, chunk, d), lambda b, t, s: (b, t, 0)),
            _layer_spec(norm_g.shape[1:], layer),
            _layer_spec(w_in.shape[1:], layer),
            _layer_spec(q_norm_g2.shape[1:], layer),
            prev_spec, cur_spec, prev_spec, cur_spec,
            _layer_spec(w_out.shape[1:], layer),
        ],
        out_specs=pl.BlockSpec((1, chunk, d), lambda b, t, s: (b, t, 0)),
        scratch_shapes=scratch,
    )
    return pl.pallas_call(
        functools.partial(_swa_kernel, sink_base=layer * N_Q_HEADS, unpermute_out=unpermute_out),
        grid_spec=grid_spec,
        out_shape=jax.ShapeDtypeStruct(x.shape, x.dtype),
        compiler_params=pltpu.CompilerParams(
            dimension_semantics=("arbitrary", "arbitrary"),
            vmem_limit_bytes=VMEM_LIMIT_BYTES),
        name="swa_out_layer" if unpermute_out else "swa_layer",
    )(sinks_flat, x, norm_g, w_in, q_norm_g2, k4, k4, v4, v4, w_out)


def _gate_group_weights(wr, wi):
    def blockdiag(w):
        n_l = w.shape[0]
        w = w.reshape(n_l, N_GATE_GROUPS, GATE_GROUP_BLOCKS, LRU_BLOCK_W, LRU_BLOCK_W)
        eye = jnp.eye(GATE_GROUP_BLOCKS, dtype=w.dtype)
        full = w[:, :, :, :, None, :] * eye[None, None, :, None, :, None]
        return full.reshape(n_l, N_GATE_GROUPS, GATE_GROUP_W, GATE_GROUP_W)
    return jnp.concatenate([blockdiag(wr), blockdiag(wi)], axis=-1).astype(BF16)


def _halve_gate_columns(w, split):
    scale = jnp.where(jnp.arange(w.shape[-1]) < split, 1.0, 0.5).astype(w.dtype)
    return (w * scale).astype(BF16)


def kernel(x, a_norm_g, a_w_in, a_conv_w, a_conv_b, a_gate_r_w, a_gate_r_b, a_gate_i_w,
           a_gate_i_b, a_lambda, a_w_out, kv_norm_g, w_kv, k_norm_g, b_norm_g, b_w_in,
           q_norm_g, sinks, b_w_out):
    n_a = a_w_in.shape[0]
    n_b = b_w_in.shape[0]
    row = lambda v: v[:, None, :]
    a_params = (
        row(a_norm_g), _halve_gate_columns(a_w_in, LRU_WIDTH), 0.5 * a_conv_w,
        row(0.5 * a_conv_b), _gate_group_weights(a_gate_r_w, a_gate_i_w),
        row(0.5 * a_gate_r_b), row(0.5 * a_gate_i_b), row(a_lambda), a_w_out.astype(BF16))
    kv_params = (kv_norm_g[None], w_kv.astype(BF16), jnp.tile(k_norm_g, N_KV_HEADS)[None])
    b_params = (row(b_norm_g), _halve_gate_columns(b_w_in, ATT_WIDTH),
                row(jnp.tile(q_norm_g, (1, 2))), b_w_out.astype(BF16))
    k4 = v4 = None
    for l in range(n_a):
        last = l == n_a - 1
        res = _rglru_layer(x, l, a_params, kv_params if last else None, permute_in=(l == 0))
        if last:
            x, k4, v4 = res
        else:
            x = res[0]
    sinks_flat = sinks.reshape(-1)
    for l in range(n_b):
        x = _swa_layer(x, k4, v4, l, b_params, sinks_flat, unpermute_out=(l == n_b - 1))
    return x
```

```python
import functools

import jax
import jax.numpy as jnp
from jax import lax
from jax.experimental import pallas as pl
from jax.experimental.pallas import tpu as pltpu

D_MODEL = 1024
LRU_WIDTH = 1536
LRU_BLOCKS = 16
LRU_BLOCK_W = LRU_WIDTH // LRU_BLOCKS
CONV_WIDTH = 4
LRU_C = 8.0
N_Q_HEADS = 16
N_KV_HEADS = 2
GROUP = N_Q_HEADS // N_KV_HEADS
HEAD_DIM = 64
ATT_WIDTH = N_Q_HEADS * HEAD_DIM
WINDOW = 128
BLOCK = 128
EPS = 1e-6

SUBLANES = 8
LANES = 128
STREAM_LEN = BLOCK // SUBLANES
CONV_TAIL_ROWS = (CONV_WIDTH - 1) * SUBLANES
SLAB_PITCH = 24
SLAB_BLOCK_ROWS = SUBLANES * SLAB_PITCH
GATE_GROUP_BLOCKS = 4
GATE_GROUP_W = GATE_GROUP_BLOCKS * LRU_BLOCK_W
N_GATE_GROUPS = LRU_BLOCKS // GATE_GROUP_BLOCKS
PAIRS_PER_KV = GROUP // 2
KV_LAYOUT_W = N_KV_HEADS * LANES

RGLRU_CHUNK = 1024
RGLRU_SUB_ROWS = 512
KV_CHUNK = 2048
KV_SUB_ROWS = 256
SWA_CHUNK = 1024
SWA_SUB_ROWS = 512
RGLRU_PIECE_W = 256
SWA_PIECE_W = 256
VMEM_LIMIT_BYTES = 56 * 1024 * 1024
F32_MIN_NORMAL = 1.17549435e-38

F32 = jnp.float32
BF16 = jnp.bfloat16


def _rmsnorm_rows(x, g):
    ms = jnp.mean(x * x, axis=-1, keepdims=True)
    return x * lax.rsqrt(ms + EPS) * g


def _softplus(z):
    return jnp.maximum(z, 0.0) + jnp.log1p(jnp.exp(-jnp.abs(z)))


def _slab_rows(n_rows):
    return n_rows // BLOCK * SLAB_BLOCK_ROWS


def _copy_time_rows(src, dst, n_rows, src_pitch, dst_pitch):
    for n in range(n_rows // BLOCK):
        for s in range(SUBLANES):
            a = n * SUBLANES * src_pitch + s * src_pitch
            b = n * SUBLANES * dst_pitch + s * dst_pitch
            dst(b, src(a))


def _load_permuted(x_ref, src_row0, slab_ref, dst_ref, dst_row0, n_rows):
    d = x_ref.shape[2]
    for c in range(d // LANES):
        lanes = slice(c * LANES, (c + 1) * LANES)

        def put(b, val, c=c):
            slab_ref[c, b:b + STREAM_LEN, :] = val
        _copy_time_rows(
            lambda a, lanes=lanes: x_ref[0, src_row0 + a:src_row0 + a + STREAM_LEN, lanes], put,
            n_rows, STREAM_LEN, SLAB_PITCH)
        for n in range(n_rows // BLOCK):
            for j in range(STREAM_LEN):
                r0 = dst_row0 + n * BLOCK + j * SUBLANES
                dst_ref[r0:r0 + SUBLANES, lanes] = (
                    slab_ref[c, pl.ds(n * SLAB_BLOCK_ROWS + j, SUBLANES, stride=SLAB_PITCH), :])


def _scatter_unpermuted(slab_ref, val, row0, lane_tile0):
    rows, w = val.shape
    for c in range(w // LANES):
        for n in range(rows // BLOCK):
            for j in range(STREAM_LEN):
                r0 = n * BLOCK + j * SUBLANES
                slab_ref[lane_tile0 + c,
                         pl.ds(_slab_rows(row0) + n * SLAB_BLOCK_ROWS + j, SUBLANES,
                               stride=SLAB_PITCH), :] = (
                    val[r0:r0 + SUBLANES, c * LANES:(c + 1) * LANES])


def _store_from_slabs(o_ref, slab_ref):
    _, t, d = o_ref.shape
    for c in range(d // LANES):
        lanes = slice(c * LANES, (c + 1) * LANES)

        def put(b, val, lanes=lanes):
            o_ref[0, b:b + STREAM_LEN, lanes] = val
        _copy_time_rows(lambda a, c=c: slab_ref[c, a:a + STREAM_LEN, :], put,
                        t, SLAB_PITCH, STREAM_LEN)


def _dup_layout(t):
    low = lax.broadcasted_iota(jnp.int32, t.shape, 1) < HEAD_DIM
    swapped = pltpu.roll(t, HEAD_DIM, axis=1)
    return jnp.concatenate([jnp.where(low, t, swapped), jnp.where(low, swapped, t)], axis=1)


def _shared_kv(x, kvg_ref, wkv_ref, kg_ref):
    h = _rmsnorm_rows(x, kvg_ref[...]).astype(BF16)
    kv = jnp.dot(h, wkv_ref[...], preferred_element_type=F32)
    kw = N_KV_HEADS * HEAD_DIM
    k = kv[:, :kw]
    v = kv[:, kw:]
    low = lax.broadcasted_iota(jnp.int32, k.shape, 1) < HEAD_DIM
    k2 = k * k
    s_low = jnp.sum(jnp.where(low, k2, 0.0), axis=-1, keepdims=True)
    s_all = jnp.sum(k2, axis=-1, keepdims=True)
    ms = jnp.where(low, s_low, s_all - s_low) * (1.0 / HEAD_DIM)
    kn = k * lax.rsqrt(ms + EPS) * kg_ref[...] * (HEAD_DIM ** -0.5)
    return _dup_layout(kn).astype(BF16), _dup_layout(v).astype(BF16)


def _recurrence_unit(u_ref, y_ref, tail_ref, hcar_ref, convw_ref, convb_ref, wg_ref, brh_ref,
                     bih_ref, half_scale, row0, grp):
    lo = grp * GATE_GROUP_W
    xh = _conv_block(u_ref, tail_ref, convw_ref, convb_ref, row0, lo)
    zh = jnp.dot(jnp.concatenate(xh, axis=0).astype(BF16), wg_ref[grp],
                 preferred_element_type=F32)
    _scan_block(zh, xh, u_ref, y_ref, hcar_ref, brh_ref, bih_ref, half_scale, row0, lo)


def _conv_block(u_ref, tail_ref, convw_ref, convb_ref, row0, lo):
    w = GATE_GROUP_W
    hi = lo + w
    rows = lambda j: slice(row0 + j * SUBLANES, row0 + (j + 1) * SUBLANES)
    x_row = lambda j: u_ref[rows(j), lo:hi]

    n_tail = CONV_WIDTH - 1
    prev_tail = tail_ref[:, lo:hi].reshape(n_tail, SUBLANES, w)
    cur_tail = jnp.concatenate(
        [x_row(STREAM_LEN - n_tail + i) for i in range(n_tail)], axis=0).reshape(
            n_tail, SUBLANES, w)
    sub3 = lax.broadcasted_iota(jnp.int32, (n_tail, SUBLANES, w), 1)
    wrapped = pltpu.roll(jnp.where(sub3 == SUBLANES - 1, prev_tail, cur_tail), 1, axis=1)
    tail_ref[:, lo:hi] = cur_tail.reshape(CONV_TAIL_ROWS, w)
    window = [wrapped[i] for i in range(n_tail)]
    taps = [convw_ref[k:k + 1, lo:hi] for k in range(CONV_WIDTH)]
    conv_b = convb_ref[:, lo:hi]
    xh = []
    for j in range(STREAM_LEN):
        xj = x_row(j)
        acc = conv_b + taps[n_tail] * xj
        for k in range(n_tail):
            acc = acc + taps[k] * window[k]
        window = window[1:] + [xj]
        xh.append(acc)
    return xh


def _scan_block(zh, xh, u_ref, y_ref, hcar_ref, brh_ref, bih_ref, half_scale, row0, lo):
    w = GATE_GROUP_W
    hi = lo + w
    rows = lambda j: slice(row0 + j * SUBLANES, row0 + (j + 1) * SUBLANES)
    brh = brh_ref[:, lo:hi]
    bih = bih_ref[:, lo:hi]
    hs_c = half_scale[:, lo:hi]
    a_rows = []
    b_rows = []
    a_acc = h_acc = None
    for j in range(STREAM_LEN):
        r = slice(j * SUBLANES, (j + 1) * SUBLANES)
        t_r = jnp.tanh(zh[r, :w] + brh)
        t_i = jnp.tanh(zh[r, w:] + bih)
        log_a = hs_c * t_r + hs_c
        a = jnp.exp(log_a)
        one_m_a2 = (-1.0 - a * a) * jnp.tanh(log_a)
        root = one_m_a2 * lax.rsqrt(jnp.maximum(one_m_a2, F32_MIN_NORMAL))
        b = (root * xh[j]) * (1.0 + t_i)
        if j == 0:
            a_acc, h_acc = a, b
        else:
            h_acc = a * h_acc + b
            a_acc = a * a_acc
        a_rows.append(a)
        b_rows.append(b)

    sub = lax.broadcasted_iota(jnp.int32, (SUBLANES, w), 0)
    shift = 1
    while shift < SUBLANES:
        keep = sub >= shift
        a_prev = jnp.where(keep, pltpu.roll(a_acc, shift, axis=0), 1.0)
        h_prev = jnp.where(keep, pltpu.roll(h_acc, shift, axis=0), 0.0)
        h_acc = a_acc * h_prev + h_acc
        a_acc = a_acc * a_prev
        shift *= 2
    h_in = hcar_ref[:, lo:hi]
    ends = a_acc * h_in + h_acc
    hcar_ref[:, lo:hi] = ends[SUBLANES - 1:SUBLANES]
    h = jnp.where(sub == 0, h_in, pltpu.roll(ends, 1, axis=0))

    ys = []
    for j in range(STREAM_LEN):
        h = a_rows[j] * h + b_rows[j]
        gh = u_ref[rows(j), LRU_WIDTH + lo:LRU_WIDTH + hi]
        ys.append(h * (gh * (1.0 + jnp.tanh(gh))))
    y_ref[row0:row0 + BLOCK, lo:hi] = jnp.concatenate(ys, axis=0).astype(BF16)


def _interleave(units, pieces):
    slots = [[] for _ in units]
    for k, piece in enumerate(pieces):
        slots[(k * len(units)) // len(pieces)].append(piece)
    for unit, before in zip(units, slots):
        for piece in before:
            piece()
        unit()


def _run_pipeline(n_sub, in_pieces, mixer_units, out_pieces):
    for piece in in_pieces(0):
        piece()
    for sb in range(n_sub):
        pieces = in_pieces(sb + 1) if sb + 1 < n_sub else []
        if sb > 0:
            pieces = pieces + out_pieces(sb - 1)
        _interleave(mixer_units(sb), pieces)
    for piece in out_pieces(n_sub - 1):
        piece()


def _rglru_kernel(x_ref, g_ref, win_ref, convw_ref, convb_ref, wg_ref, brh_ref, bih_ref, lam_ref,
                  wout_ref, o_ref, tail_ref, hcar_ref, hb_ref, *scratch, permute_in):
    u_slots, y_slots = scratch[0:2], scratch[2:4]
    chunk = x_ref.shape[1]
    sub_rows = RGLRU_SUB_ROWS
    piece_w = RGLRU_PIECE_W
    n_sub = chunk // sub_rows

    if permute_in:
        slab_ref, xp_ref = scratch[4:6]
        x_rows = lambda r0, cols=slice(None): xp_ref[r0:r0 + sub_rows, cols]
    else:
        x_rows = lambda r0, cols=slice(None): x_ref[0, r0:r0 + sub_rows, cols]

    half_scale = (-0.5 * LRU_C) * _softplus(-lam_ref[...])

    def in_proj_pieces(sb):
        r0 = sb * sub_rows
        slot = sb % 2

        def piece(c):
            if c == 0:
                if permute_in:
                    _load_permuted(x_ref, r0, slab_ref, xp_ref, r0, sub_rows)
                hb_ref[...] = _rmsnorm_rows(x_rows(r0), g_ref[...]).astype(BF16)
            cols = slice(c * piece_w, (c + 1) * piece_w)
            u_slots[slot][:, cols] = jnp.dot(hb_ref[...], win_ref[:, cols],
                                             preferred_element_type=F32)
        return [functools.partial(piece, c) for c in range(2 * LRU_WIDTH // piece_w)]

    def recurrence_units(sb):
        slot = sb % 2
        return [functools.partial(
            _recurrence_unit, u_slots[slot], y_slots[slot], tail_ref, hcar_ref, convw_ref,
            convb_ref, wg_ref, brh_ref, bih_ref, half_scale, n * BLOCK, grp)
            for n in range(sub_rows // BLOCK) for grp in range(N_GATE_GROUPS)]

    def out_proj_pieces(sb):
        r0 = sb * sub_rows
        slot = sb % 2

        def piece(c):
            cols = slice(c * piece_w, (c + 1) * piece_w)
            o_ref[0, r0:r0 + sub_rows, cols] = (
                x_rows(r0, cols)
                + jnp.dot(y_slots[slot][...], wout_ref[:, cols], preferred_element_type=F32))

        return [functools.partial(piece, c) for c in range(D_MODEL // piece_w)]

    @pl.when(pl.program_id(1) == 0)
    def _():
        tail_ref[...] = jnp.zeros_like(tail_ref)
        hcar_ref[...] = jnp.zeros_like(hcar_ref)

    _run_pipeline(n_sub, in_proj_pieces, recurrence_units, out_proj_pieces)


def _layer_spec(shape, layer):
    zeros = (0,) * len(shape)
    return pl.BlockSpec((None,) + tuple(shape), lambda *_: (layer,) + zeros,
                        pipeline_mode=pl.Buffered(1))


def _const_spec(shape):
    zeros = (0,) * len(shape)
    return pl.BlockSpec(tuple(shape), lambda *_: zeros, pipeline_mode=pl.Buffered(1))


def _rglru_layer(x, layer, params, *, permute_in):
    bsz, seq, d = x.shape
    chunk = RGLRU_CHUNK
    sub_rows = RGLRU_SUB_ROWS
    tile = pl.BlockSpec((1, chunk, d), lambda b, t: (b, t, 0))
    scratch = [
        pltpu.VMEM((CONV_TAIL_ROWS, LRU_WIDTH), F32),
        pltpu.VMEM((1, LRU_WIDTH), F32),
        pltpu.VMEM((sub_rows, d), BF16),
    ]
    scratch += [pltpu.VMEM((sub_rows, 2 * LRU_WIDTH), F32)] * 2
    scratch += [pltpu.VMEM((sub_rows, LRU_WIDTH), BF16)] * 2
    if permute_in:
        scratch += [pltpu.VMEM((d // LANES, _slab_rows(sub_rows), LANES), F32),
                    pltpu.VMEM((chunk, d), F32)]
    return pl.pallas_call(
        functools.partial(_rglru_kernel, permute_in=permute_in),
        grid=(bsz, seq // chunk),
        in_specs=[tile] + [_layer_spec(p.shape[1:], layer) for p in params],
        out_specs=tile,
        out_shape=jax.ShapeDtypeStruct(x.shape, x.dtype),
        scratch_shapes=scratch,
        compiler_params=pltpu.CompilerParams(
            dimension_semantics=("arbitrary", "arbitrary"),
            vmem_limit_bytes=VMEM_LIMIT_BYTES),
        name="rglru_layer",
    )(x, *params)


def _kv_kernel(x_ref, kvg_ref, wkv_ref, kg_ref, k_ref, v_ref):
    for r0 in range(0, x_ref.shape[1], KV_SUB_ROWS):
        k2, v2 = _shared_kv(x_ref[0, r0:r0 + KV_SUB_ROWS, :], kvg_ref, wkv_ref, kg_ref)
        k_ref[0, r0:r0 + KV_SUB_ROWS, :] = k2
        v_ref[0, r0:r0 + KV_SUB_ROWS, :] = v2


def _kv_layer(x, kv_params):
    bsz, seq, d = x.shape
    tile = lambda w: pl.BlockSpec((1, KV_CHUNK, w), lambda b, t: (b, t, 0))
    return pl.pallas_call(
        _kv_kernel,
        grid=(bsz, seq // KV_CHUNK),
        in_specs=[tile(d)] + [_const_spec(p.shape) for p in kv_params],
        out_specs=[tile(KV_LAYOUT_W)] * 2,
        out_shape=[jax.ShapeDtypeStruct((bsz, seq, KV_LAYOUT_W), BF16)] * 2,
        compiler_params=pltpu.CompilerParams(
            dimension_semantics=("arbitrary", "arbitrary"),
            vmem_limit_bytes=VMEM_LIMIT_BYTES),
        name="shared_kv",
    )(x, *kv_params)


def _attention_unit(qn_ref, sg_ref, y_ref, kprev_ref, kcur_ref, vprev_ref, vcur_ref, sinks_ref,
                    sink_base, masks, first_chunk, row0, blk, j):
    from_prev, dist, low_o, ones = masks
    rows = slice(row0, row0 + BLOCK)
    pair_lanes = lambda i: slice((j * PAIRS_PER_KV + i) * LANES, (j * PAIRS_PER_KV + i + 1) * LANES)
    cols = slice(j * LANES, (j + 1) * LANES)

    def band(prev_ref, cur_ref):
        if blk == 0:
            return jnp.concatenate([prev_ref[0, :, cols], cur_ref[0, 0:BLOCK, cols]], axis=0)
        return cur_ref[0, (blk - 1) * BLOCK:(blk + 1) * BLOCK, cols]

    zero_q = jnp.zeros((BLOCK, LANES), BF16)
    q_pairs = [qn_ref[rows, pair_lanes(i)] for i in range(PAIRS_PER_KV)]
    q_stack = jnp.concatenate(
        [jnp.where(low_o, q, zero_q) for q in q_pairs]
        + [jnp.where(low_o, zero_q, q) for q in q_pairs], axis=0)
    s = lax.dot_general(q_stack, band(kprev_ref, kcur_ref), (((1,), (1,)), ((), ())),
                        preferred_element_type=F32)
    probs = []
    row_max = []
    for e in range(2):
        for i in range(PAIRS_PER_KV):
            head = j * GROUP + 2 * i + e
            slope = 2.0 ** (-8.0 * (head + 1) / N_Q_HEADS)
            r = e * PAIRS_PER_KV + i
            s_h = s[r * BLOCK:(r + 1) * BLOCK]
            sc = jnp.where(from_prev, s_h[:, :BLOCK], s_h[:, BLOCK:]) - slope * dist
            if blk == 0:
                sc = jnp.where(jnp.logical_and(from_prev, first_chunk), -jnp.inf, sc)
            m = jnp.maximum(jnp.max(sc, axis=-1, keepdims=True), sinks_ref[sink_base + head])
            pr = jnp.exp(sc - m)
            zero = jnp.zeros_like(pr)
            probs.append(jnp.concatenate(
                [jnp.where(from_prev, pr, zero), jnp.where(from_prev, zero, pr)],
                axis=1).astype(BF16))
            row_max.append(m)
    v_ext = jnp.concatenate([band(vprev_ref, vcur_ref), ones], axis=1)
    acc = jnp.dot(jnp.concatenate(probs, axis=0), v_ext, preferred_element_type=F32)
    for i in range(PAIRS_PER_KV):
        h_even = j * GROUP + 2 * i
        even = acc[i * BLOCK:(i + 1) * BLOCK]
        odd = acc[(PAIRS_PER_KV + i) * BLOCK:(PAIRS_PER_KV + i + 1) * BLOCK]
        m_pair = jnp.where(low_o, row_max[i], row_max[PAIRS_PER_KV + i])
        sink_pair = jnp.where(low_o, sinks_ref[sink_base + h_even],
                              sinks_ref[sink_base + h_even + 1])
        den = jnp.where(low_o, even[:, LANES:], odd[:, LANES:]) + jnp.exp(sink_pair - m_pair)
        o = jnp.where(low_o, even[:, :LANES], odd[:, :LANES]) * (1.0 / den)
        y_ref[rows, pair_lanes(i)] = (o * sg_ref[rows, pair_lanes(i)]).astype(BF16)


def _swa_kernel(sinks_ref, x_ref, g_ref, win_ref, qg_ref, kprev_ref, kcur_ref, vprev_ref,
                vcur_ref, wout_ref, o_ref, hb_ref, *scratch, sink_base, unpermute_out):
    chunk = x_ref.shape[1]
    sub_rows = SWA_SUB_ROWS
    piece_w = SWA_PIECE_W
    qn_slots, sg_slots, y_slots, extra = scratch[0:2], scratch[2:4], scratch[4:6], scratch[6:]
    first_chunk = pl.program_id(1) == 0

    ri = lax.broadcasted_iota(jnp.int32, (BLOCK, BLOCK), 0)
    ci = lax.broadcasted_iota(jnp.int32, (BLOCK, BLOCK), 1)
    q_time = (ri & (SUBLANES - 1)) * STREAM_LEN + (ri >> 3)
    k_time = (ci & (SUBLANES - 1)) * STREAM_LEN + (ci >> 3)
    from_prev = k_time > q_time
    dist = ((q_time - k_time) & (BLOCK - 1)).astype(F32)
    low_o = ci < HEAD_DIM
    masks = (from_prev, dist, low_o, jnp.ones((2 * BLOCK, LANES), BF16))
    low_q = lax.broadcasted_iota(jnp.int32, (sub_rows, LANES), 1) < HEAD_DIM

    def in_proj_pieces(sb):
        r0 = sb * sub_rows
        slot = sb % 2

        def piece(c):
            if c == 0:
                hb_ref[...] = _rmsnorm_rows(x_ref[0, r0:r0 + sub_rows, :], g_ref[...]).astype(BF16)
            cols = slice(c * piece_w, (c + 1) * piece_w)
            u = jnp.dot(hb_ref[...], win_ref[:, cols], preferred_element_type=F32)
            if c < ATT_WIDTH // piece_w:
                for p in range(piece_w // LANES):
                    q = u[:, p * LANES:(p + 1) * LANES]
                    q2 = q * q
                    s_low = jnp.sum(jnp.where(low_q, q2, 0.0), axis=-1, keepdims=True)
                    s_all = jnp.sum(q2, axis=-1, keepdims=True)
                    ms = jnp.where(low_q, s_low, s_all - s_low) * (1.0 / HEAD_DIM)
                    qn = q * lax.rsqrt(ms + EPS) * qg_ref[...]
                    l0 = c * piece_w + p * LANES
                    qn_slots[slot][:, l0:l0 + LANES] = qn.astype(BF16)
            else:
                lo = c * piece_w - ATT_WIDTH
                sg_slots[slot][:, lo:lo + piece_w] = u * (1.0 + jnp.tanh(u))
        return [functools.partial(piece, c) for c in range(2 * ATT_WIDTH // piece_w)]

    def attention_units(sb):
        slot = sb % 2
        return [functools.partial(
            _attention_unit, qn_slots[slot], sg_slots[slot], y_slots[slot], kprev_ref,
            kcur_ref, vprev_ref, vcur_ref, sinks_ref, sink_base, masks, first_chunk,
            n * BLOCK, sb * (sub_rows // BLOCK) + n, j)
            for n in range(sub_rows // BLOCK) for j in range(N_KV_HEADS)]

    def out_proj_pieces(sb):
        r0 = sb * sub_rows
        slot = sb % 2

        def piece(c):
            cols = slice(c * piece_w, (c + 1) * piece_w)
            out = (x_ref[0, r0:r0 + sub_rows, cols]
                   + jnp.dot(y_slots[slot][...], wout_ref[:, cols], preferred_element_type=F32))
            if unpermute_out:
                _scatter_unpermuted(extra[0], out, r0, c * piece_w // LANES)
            else:
                o_ref[0, r0:r0 + sub_rows, cols] = out
        return [functools.partial(piece, c) for c in range(D_MODEL // piece_w)]

    _run_pipeline(chunk // sub_rows, in_proj_pieces, attention_units, out_proj_pieces)
    if unpermute_out:
        _store_from_slabs(o_ref, extra[0])


def _swa_layer(x, k4, v4, layer, params, sinks_flat, *, unpermute_out):
    bsz, seq, d = x.shape
    chunk = SWA_CHUNK
    sub_rows = SWA_SUB_ROWS
    blocks_per_chunk = chunk // BLOCK
    norm_g, w_in, q_norm_g2, w_out = params
    prev_spec = pl.BlockSpec(
        (1, BLOCK, KV_LAYOUT_W),
        lambda b, t, s: (b, jnp.maximum(t * blocks_per_chunk - 1, 0), 0))
    cur_spec = pl.BlockSpec((1, chunk, KV_LAYOUT_W), lambda b, t, s: (b, t, 0))
    scratch = [
        pltpu.VMEM((sub_rows, d), BF16),
    ]
    scratch += [pltpu.VMEM((sub_rows, ATT_WIDTH), BF16)] * 2
    scratch += [pltpu.VMEM((sub_rows, ATT_WIDTH), F32)] * 2
    scratch += [pltpu.VMEM((sub_rows, ATT_WIDTH), BF16)] * 2
    if unpermute_out:
        scratch.append(pltpu.VMEM((d // LANES, _slab_rows(chunk), LANES), F32))
    grid_spec = pltpu.PrefetchScalarGridSpec(
        num_scalar_prefetch=1,
        grid=(bsz, seq // chunk),
        in_specs=[
            pl.BlockSpec((1, chunk, d), lambda b, t, s: (b, t, 0)),
            _layer_spec(norm_g.shape[1:], layer),
            _layer_spec(w_in.shape[1:], layer),
            _layer_spec(q_norm_g2.shape[1:], layer),
            prev_spec, cur_spec, prev_spec, cur_spec,
            _layer_spec(w_out.shape[1:], layer),
        ],
        out_specs=pl.BlockSpec((1, chunk, d), lambda b, t, s: (b, t, 0)),
        scratch_shapes=scratch,
    )
    return pl.pallas_call(
        functools.partial(_swa_kernel, sink_base=layer * N_Q_HEADS, unpermute_out=unpermute_out),
        grid_spec=grid_spec,
        out_shape=jax.ShapeDtypeStruct(x.shape, x.dtype),
        compiler_params=pltpu.CompilerParams(
            dimension_semantics=("arbitrary", "arbitrary"),
            vmem_limit_bytes=VMEM_LIMIT_BYTES),
        name="swa_out_layer" if unpermute_out else "swa_layer",
    )(sinks_flat, x, norm_g, w_in, q_norm_g2, k4, k4, v4, v4, w_out)


def _gate_group_weights(wr, wi):
    def blockdiag(w):
        n_l = w.shape[0]
        w = w.reshape(n_l, N_GATE_GROUPS, GATE_GROUP_BLOCKS, LRU_BLOCK_W, LRU_BLOCK_W)
        eye = jnp.eye(GATE_GROUP_BLOCKS, dtype=w.dtype)
        full = w[:, :, :, :, None, :] * eye[None, None, :, None, :, None]
        return full.reshape(n_l, N_GATE_GROUPS, GATE_GROUP_W, GATE_GROUP_W)
    return jnp.concatenate([blockdiag(wr), blockdiag(wi)], axis=-1).astype(BF16)


def _halve_gate_columns(w, split):
    scale = jnp.where(jnp.arange(w.shape[-1]) < split, 1.0, 0.5).astype(w.dtype)
    return (w * scale).astype(BF16)


def kernel(x, a_norm_g, a_w_in, a_conv_w, a_conv_b, a_gate_r_w, a_gate_r_b, a_gate_i_w,
           a_gate_i_b, a_lambda, a_w_out, kv_norm_g, w_kv, k_norm_g, b_norm_g, b_w_in,
           q_norm_g, sinks, b_w_out):
    n_a = a_w_in.shape[0]
    n_b = b_w_in.shape[0]
    row = lambda v: v[:, None, :]
    a_params = (
        row(a_norm_g), _halve_gate_columns(a_w_in, LRU_WIDTH), 0.5 * a_conv_w,
        row(0.5 * a_conv_b), _gate_group_weights(a_gate_r_w, a_gate_i_w),
        row(0.5 * a_gate_r_b), row(0.5 * a_gate_i_b), row(a_lambda), a_w_out.astype(BF16))
    kv_params = (kv_norm_g[None], w_kv.astype(BF16), jnp.tile(k_norm_g, N_KV_HEADS)[None])
    b_params = (row(b_norm_g), _halve_gate_columns(b_w_in, ATT_WIDTH),
                row(jnp.tile(q_norm_g, (1, 2))), b_w_out.astype(BF16))
    for l in range(n_a):
        x = _rglru_layer(x, l, a_params, permute_in=(l == 0))
    k4, v4 = _kv_layer(x, kv_params)
    sinks_flat = sinks.reshape(-1)
    for l in range(n_b):
        x = _swa_layer(x, k4, v4, l, b_params, sinks_flat, unpermute_out=(l == n_b - 1))
    return x
```
